```python
import math
import jax, jax.numpy as jnp
from jax import lax
import numpy as np

D_MODEL = 1024
BATCH = 8
SEQ = 8192
DEPTH = 2

A_EXPAND = 2
A_WIDTH = A_EXPAND * D_MODEL
A_GROUPS = 8
A_GROUP_DIM = A_WIDTH // A_GROUPS
CHUNK = 128
HEAD_DIM = 64
Q_HEADS = D_MODEL // HEAD_DIM
KV_HEADS = 2
Q_PER_KV = Q_HEADS // KV_HEADS
WINDOW = 128
ROPE_THETA = 10000.0
EPS = 1e-6

kernel_name = "yoco_gmlp_swa_sink_hybrid"


def rms_norm(x, g):
    xf = x.astype(jnp.float32)
    y = xf * lax.rsqrt(jnp.mean(xf * xf, axis=-1, keepdims=True) + EPS)
    return (y * g.astype(jnp.float32)).astype(x.dtype)


def layer_norm(x, g, b):
    xf = x.astype(jnp.float32)
    mu = jnp.mean(xf, axis=-1, keepdims=True)
    var = jnp.mean(jnp.square(xf - mu), axis=-1, keepdims=True)
    y = (xf - mu) * lax.rsqrt(var + EPS)
    return (y * g.astype(jnp.float32) + b.astype(jnp.float32)).astype(x.dtype)


def adaln(c, w, b):
    m = jax.nn.silu(c) @ w + b
    shift, scale, gate = jnp.split(m, 3, axis=-1)
    return shift[:, None, :], scale[:, None, :], gate[:, None, :]


def rope_tables(positions, dtype):
    inv_freq = ROPE_THETA ** (-jnp.arange(0, HEAD_DIM, 2, dtype=jnp.float32) / HEAD_DIM)
    ang = positions.astype(jnp.float32)[..., None] * inv_freq
    return jnp.cos(ang).astype(dtype)[:, :, None, :], jnp.sin(ang).astype(dtype)[:, :, None, :]


def apply_rope(x, cos, sin):
    x1, x2 = jnp.split(x, 2, axis=-1)
    return jnp.concatenate([x1 * cos - x2 * sin, x2 * cos + x1 * sin], axis=-1)


def mixer_a(h, w_in, ln_g, ln_b, w_s, b_s, w_out):
    B, S, _ = h.shape
    u, v, z = jnp.split(h @ w_in, 3, axis=-1)
    u = jax.nn.gelu(u)
    v = layer_norm(jax.nn.gelu(v), ln_g, ln_b)
    vc = v.reshape(B, S // CHUNK, CHUNK, A_GROUPS, A_GROUP_DIM)
    causal = jnp.tril(jnp.ones((CHUNK, CHUNK), dtype=w_s.dtype))
    ws = w_s * causal[None]
    mixed = jnp.einsum("gts,bnsgc->bntgc", ws, vc) + b_s.T[:, :, None]
    y = u * mixed.reshape(B, S, A_WIDTH) * jax.nn.silu(z)
    return y @ w_out


def band_blocks(t):
    B, S, K, Dh = t.shape
    tb = t.reshape(B, S // WINDOW, WINDOW, K, Dh)
    prev = jnp.concatenate([jnp.zeros_like(tb[:, :1]), tb[:, :-1]], axis=1)
    return jnp.concatenate([prev, tb], axis=2)


def band_mask(n_blocks):
    qi = jnp.arange(WINDOW)[:, None]
    s = jnp.arange(2 * WINDOW)[None, :]
    diff = WINDOW + qi - s
    key_pos = jnp.arange(n_blocks)[:, None, None] * WINDOW - WINDOW + s[None]
    valid = (diff >= 0) & (diff < WINDOW)
    return valid[None] & (key_pos >= 0)


def mixer_b(h, k_band, v_band, cos, sin, w_in, sinks, w_out):
    B, S, _ = h.shape
    nblk = S // WINDOW
    q, z = jnp.split(h @ w_in, 2, axis=-1)
    q = apply_rope(q.reshape(B, S, Q_HEADS, HEAD_DIM), cos, sin)
    q = q.reshape(B, nblk, WINDOW, KV_HEADS, Q_PER_KV, HEAD_DIM)
    logits = jnp.einsum("bnqkgd,bnskd->bnkgqs", q, k_band).astype(jnp.float32)
    logits = logits * (1.0 / math.sqrt(HEAD_DIM))
    mask = band_mask(nblk)[None, :, None, None]
    logits = jnp.where(mask, logits, jnp.float32(-1e30))
    sink = jnp.broadcast_to(sinks.astype(jnp.float32).reshape(1, 1, KV_HEADS, Q_PER_KV, 1, 1),
                            logits.shape[:-1] + (1,))
    probs = jax.nn.softmax(jnp.concatenate([logits, sink], axis=-1), axis=-1)[..., :-1]
    o = jnp.einsum("bnkgqs,bnskd->bnqkgd", probs.astype(v_band.dtype), v_band)
    o = o.reshape(B, S, Q_HEADS * HEAD_DIM)
    return (o * jax.nn.silu(z)) @ w_out


def setup_inputs(seed: int = 0) -> dict:
    key = jax.random.key(seed)
    ks = jax.random.split(key, 24)
    na = DEPTH // 2
    nb = DEPTH - na
    D = D_MODEL
    f32 = jnp.float32
    nrm = lambda k, shape, s: jax.random.normal(k, shape, f32) * s
    x = nrm(ks[0], (BATCH, SEQ, D), 1.0)
    c = nrm(ks[1], (BATCH, D), 1.0)
    positions = jnp.broadcast_to(jnp.arange(SEQ, dtype=jnp.int32)[None, :], (BATCH, SEQ))
    return {
        "x": x,
        "c": c,
        "positions": positions,
        "a_norm": 1.0 + nrm(ks[2], (na, D), 0.02),
        "a_ada_w": nrm(ks[3], (na, D, 3 * D), 0.5 * D ** -0.5),
        "a_ada_b": nrm(ks[4], (na, 3 * D), 0.01),
        "a_w_in": nrm(ks[5], (na, D, 3 * A_WIDTH), D ** -0.5),
        "a_ln_g": 1.0 + nrm(ks[6], (na, A_WIDTH), 0.02),
        "a_ln_b": nrm(ks[7], (na, A_WIDTH), 0.02),
        "a_w_s": nrm(ks[8], (na, A_GROUPS, CHUNK, CHUNK), CHUNK ** -0.5),
        "a_b_s": 1.0 + nrm(ks[9], (na, A_GROUPS, CHUNK), 0.02),
        "a_w_out": nrm(ks[10], (na, A_WIDTH, D), A_WIDTH ** -0.5),
        "kv_norm": 1.0 + nrm(ks[11], (D,), 0.02),
        "w_kv": nrm(ks[12], (D, 2 * KV_HEADS * HEAD_DIM), D ** -0.5),
        "b_norm": 1.0 + nrm(ks[13], (nb, D), 0.02),
        "b_ada_w": nrm(ks[14], (nb, D, 3 * D), 0.5 * D ** -0.5),
        "b_ada_b": nrm(ks[15], (nb, 3 * D), 0.01),
        "b_w_in": nrm(ks[16], (nb, D, 2 * Q_HEADS * HEAD_DIM), D ** -0.5),
        "b_sinks": nrm(ks[17], (nb, Q_HEADS), 0.5),
        "b_w_out": nrm(ks[18], (nb, Q_HEADS * HEAD_DIM, D), (Q_HEADS * HEAD_DIM) ** -0.5),
        "final_norm": 1.0 + nrm(ks[19], (D,), 0.02),
    }


def reference(x, c, positions, a_norm, a_ada_w, a_ada_b, a_w_in, a_ln_g, a_ln_b, a_w_s,
              a_b_s, a_w_out, kv_norm, w_kv, b_norm, b_ada_w, b_ada_b, b_w_in, b_sinks,
              b_w_out, final_norm):
    B, S, _ = x.shape
    na = DEPTH // 2
    cos, sin = rope_tables(positions, x.dtype)
    k_band = v_band = None
    for layer in range(DEPTH):
        if layer < na:
            i = layer
            shift, scale, gate = adaln(c, a_ada_w[i], a_ada_b[i])
            h = rms_norm(x, a_norm[i]) * (1.0 + scale) + shift
            x = x + gate * mixer_a(h, a_w_in[i], a_ln_g[i], a_ln_b[i], a_w_s[i], a_b_s[i], a_w_out[i])
        else:
            i = layer - na
            if i == 0:
                kv = rms_norm(x, kv_norm) @ w_kv
                k, v = jnp.split(kv, 2, axis=-1)
                k = apply_rope(k.reshape(B, S, KV_HEADS, HEAD_DIM), cos, sin)
                v = v.reshape(B, S, KV_HEADS, HEAD_DIM)
                k_band, v_band = band_blocks(k), band_blocks(v)
            shift, scale, gate = adaln(c, b_ada_w[i], b_ada_b[i])
            h = rms_norm(x, b_norm[i]) * (1.0 + scale) + shift
            x = x + gate * mixer_b(h, k_band, v_band, cos, sin, b_w_in[i], b_sinks[i], b_w_out[i])
    return rms_norm(x, final_norm)
```

```python
import functools
import math

import jax
import jax.numpy as jnp
from jax import lax
from jax.experimental import pallas as pl
from jax.experimental.pallas import tpu as pltpu

F32 = jnp.float32
BF16 = jnp.bfloat16

D_MODEL = 1024
A_WIDTH = 2048
A_GROUPS = 8
A_GROUP_DIM = A_WIDTH // A_GROUPS
CHUNK = 128
HEAD_DIM = 64
HALF_DIM = HEAD_DIM // 2
Q_HEADS = 16
KV_HEADS = 2
Q_PER_KV = Q_HEADS // KV_HEADS
WINDOW = 128
ROPE_THETA = 10000.0
EPS = 1e-6

LANES = 128
ROW_TILE = 512
POS_PER_ROW = LANES // HALF_DIM
TABLE_ROWS = ROW_TILE // POS_PER_ROW
VMEM_LIMIT_BYTES = 56 * 1024 * 1024


def _const_spec(shape):
    zeros = (0,) * len(shape)
    return pl.BlockSpec(shape, lambda *_: zeros, pipeline_mode=pl.Buffered(1))


def _prep_kernel(c_ref, wa_ref, ba_ref, wb_ref, bb_ref, ws_ref, moda_ref, modb_ref, wsm_ref):
    sc = jax.nn.silu(c_ref[...])
    moda_ref[...] = jnp.dot(sc, wa_ref[0], preferred_element_type=F32,
                            precision=lax.Precision.HIGHEST) + ba_ref[0]
    modb_ref[...] = jnp.dot(sc, wb_ref[0], preferred_element_type=F32,
                            precision=lax.Precision.HIGHEST) + bb_ref[0]
    t = lax.broadcasted_iota(jnp.int32, (CHUNK, CHUNK), 0)
    s = lax.broadcasted_iota(jnp.int32, (CHUNK, CHUNK), 1)
    causal = (s <= t).astype(F32)
    for g in range(A_GROUPS):
        wsm_ref[g] = (ws_ref[0, g] * causal).astype(BF16)


def _prep(c, a_ada_w, a_ada_b, b_ada_w, b_ada_b, a_w_s):
    batch = c.shape[0]
    col = 1024
    n_col = 3 * D_MODEL // col
    return pl.pallas_call(
        _prep_kernel,
        grid=(n_col,),
        in_specs=[
            pl.BlockSpec((batch, D_MODEL), lambda j: (0, 0)),
            pl.BlockSpec((1, D_MODEL, col), lambda j: (0, 0, j)),
            pl.BlockSpec((1, 1, col), lambda j: (0, 0, j)),
            pl.BlockSpec((1, D_MODEL, col), lambda j: (0, 0, j)),
            pl.BlockSpec((1, 1, col), lambda j: (0, 0, j)),
            pl.BlockSpec((1, A_GROUPS, CHUNK, CHUNK), lambda j: (0, 0, 0, 0)),
        ],
        out_specs=[
            pl.BlockSpec((batch, col), lambda j: (0, j)),
            pl.BlockSpec((batch, col), lambda j: (0, j)),
            pl.BlockSpec((A_GROUPS, CHUNK, CHUNK), lambda j: (0, 0, 0)),
        ],
        out_shape=[
            jax.ShapeDtypeStruct((batch, 3 * D_MODEL), F32),
            jax.ShapeDtypeStruct((batch, 3 * D_MODEL), F32),
            jax.ShapeDtypeStruct((A_GROUPS, CHUNK, CHUNK), BF16),
        ],
        compiler_params=pltpu.CompilerParams(dimension_semantics=("arbitrary",)),
        name="prep",
    )(c, a_ada_w, a_ada_b.reshape(1, 1, -1), b_ada_w, b_ada_b.reshape(1, 1, -1), a_w_s)


def _rope_kernel(pos_ref, invf_ref, cos_ref, sin_ref):
    ang = pos_ref[...].astype(F32) * invf_ref[...]
    cos_ref[...] = jnp.cos(ang)
    sin_ref[...] = jnp.sin(ang)


def _rope_tables(positions):
    n = positions.size
    inv_freq = ROPE_THETA ** (-jnp.arange(0, HEAD_DIM, 2, dtype=F32) / HEAD_DIM)
    invf = jnp.tile(inv_freq, POS_PER_ROW).reshape(1, LANES)
    pos = positions.reshape(n // ROW_TILE, POS_PER_ROW, TABLE_ROWS)
    pos = jnp.repeat(jnp.swapaxes(pos, 1, 2), HALF_DIM, axis=2).reshape(n // POS_PER_ROW, LANES)
    rows = 2048
    return pl.pallas_call(
        _rope_kernel,
        grid=(pos.shape[0] // rows,),
        in_specs=[pl.BlockSpec((rows, LANES), lambda i: (i, 0)),
                  pl.BlockSpec((1, LANES), lambda i: (0, 0))],
        out_specs=[pl.BlockSpec((rows, LANES), lambda i: (i, 0)),
                   pl.BlockSpec((rows, LANES), lambda i: (i, 0))],
        out_shape=[jax.ShapeDtypeStruct(pos.shape, F32), jax.ShapeDtypeStruct(pos.shape, F32)],
        compiler_params=pltpu.CompilerParams(dimension_semantics=("arbitrary",)),
        name="rope_tables",
    )(pos, invf)


def _expand_tables(cos_c, sin_c):
    lane = lax.broadcasted_iota(jnp.int32, (TABLE_ROWS, LANES), 1)
    first_half = (lane & (HEAD_DIM - 1)) < HALF_DIM

    def spread(t):
        t = jnp.where(lane < HALF_DIM, t, pltpu.roll(t, HALF_DIM, 1))
        return jnp.where(lane < HEAD_DIM, t, pltpu.roll(t, HEAD_DIM, 1))

    cos_parts, sin_parts = [], []
    for a in range(POS_PER_ROW):
        shift = (LANES - HALF_DIM * a) % LANES
        c = cos_c if shift == 0 else pltpu.roll(cos_c, shift, 1)
        s = sin_c if shift == 0 else pltpu.roll(sin_c, shift, 1)
        cos_parts.append(spread(c))
        s = spread(s)
        sin_parts.append(jnp.where(first_half, -s, s))
    return jnp.concatenate(cos_parts, axis=0), jnp.concatenate(sin_parts, axis=0)


def _rotate_half_pairs(x):
    lane = lax.broadcasted_iota(jnp.int32, x.shape, 1)
    first_half = (lane & (HEAD_DIM - 1)) < HALF_DIM
    return jnp.where(first_half, pltpu.roll(x, LANES - HALF_DIM, 1), pltpu.roll(x, HALF_DIM, 1))


def _layer_a_kernel(x_ref, mod_ref, anorm_ref, win_ref, lng_ref, lnb_ref, ws_ref, bs_ref, wout_ref,
                    kvn_ref, wkv_ref, cosc_ref, sinc_ref,
                    x1_ref, kd_ref, vd_ref,
                    h_scr, gv_scr, y_scr, s1_scr, s2_scr):
    x = x_ref[...]
    mod = mod_ref[0]
    shift = mod[:, 0:D_MODEL]
    scale = mod[:, D_MODEL:2 * D_MODEL]
    gate = mod[:, 2 * D_MODEL:3 * D_MODEL]
    rstd = lax.rsqrt(jnp.mean(x * x, axis=-1, keepdims=True) + EPS)
    gs = anorm_ref[...] * (1.0 + scale)
    h_scr[...] = ((x * rstd) * gs + shift).astype(BF16)

    s1_scr[...] = jnp.zeros_like(s1_scr)
    s2_scr[...] = jnp.zeros_like(s2_scr)

    def stats_body(g, carry):
        v = jnp.dot(h_scr[...], win_ref[1, g], preferred_element_type=F32)
        gv = jax.nn.gelu(v)
        gv_scr[g] = gv
        lo = gv[:, :LANES]
        hi = gv[:, LANES:]
        s1_scr[...] += lo + hi
        s2_scr[...] += lo * lo + hi * hi
        return carry

    lax.fori_loop(0, A_GROUPS, stats_body, 0)
    mu = jnp.sum(s1_scr[...], axis=-1, keepdims=True) * (1.0 / A_WIDTH)
    var = jnp.sum(s2_scr[...], axis=-1, keepdims=True) * (1.0 / A_WIDTH) - mu * mu
    s1_scr[...] = jnp.broadcast_to(mu, s1_scr.shape)
    s2_scr[...] = jnp.broadcast_to(lax.rsqrt(var + EPS), s2_scr.shape)

    def mix_body(g, carry):
        mu_b = s1_scr[...]
        rs_b = s2_scr[...]
        gv = gv_scr[g]
        lg = lng_ref[g]
        lb = lnb_ref[g]
        vn = jnp.concatenate(
            [((gv[:, :LANES] - mu_b) * rs_b) * lg[:, :LANES] + lb[:, :LANES],
             ((gv[:, LANES:] - mu_b) * rs_b) * lg[:, LANES:] + lb[:, LANES:]], axis=1).astype(BF16)
        h = h_scr[...]
        u = jnp.dot(h, win_ref[0, g], preferred_element_type=F32)
        z = jnp.dot(h, win_ref[2, g], preferred_element_type=F32)
        ws = ws_ref[g]
        bs = bs_ref[g]
        bs2 = jnp.concatenate([bs, bs], axis=1)
        for c in range(ROW_TILE // CHUNK):
            rows = slice(c * CHUNK, (c + 1) * CHUNK)
            mixed = jnp.dot(ws, vn[rows], preferred_element_type=F32) + bs2
            y = jax.nn.gelu(u[rows]) * mixed * jax.nn.silu(z[rows])
            y_scr[g, rows, :] = y.astype(BF16)
        return carry

    lax.fori_loop(0, A_GROUPS, mix_body, 0)

    out = jnp.dot(y_scr[0], wout_ref[0], preferred_element_type=F32)
    for g in range(1, A_GROUPS):
        out += jnp.dot(y_scr[g], wout_ref[g], preferred_element_type=F32)
    x1 = x + gate * out
    x1_ref[...] = x1

    rstd2 = lax.rsqrt(jnp.mean(x1 * x1, axis=-1, keepdims=True) + EPS)
    hk = ((x1 * rstd2) * kvn_ref[...]).astype(BF16)
    kv = jnp.dot(hk, wkv_ref[...], preferred_element_type=F32)
    k = kv[:, :LANES]
    v = kv[:, LANES:]
    cos_t, sin_t = _expand_tables(cosc_ref[...], sinc_ref[...])
    k = k * cos_t + _rotate_half_pairs(k) * sin_t
    lane = lax.broadcasted_iota(jnp.int32, k.shape, 1)
    low = lane < HEAD_DIM
    k_sw = pltpu.roll(k, HEAD_DIM, 1)
    v_sw = pltpu.roll(v, HEAD_DIM, 1)
    kd_ref[0] = jnp.where(low, k, k_sw).astype(BF16)
    kd_ref[1] = jnp.where(low, k_sw, k).astype(BF16)
    vd_ref[0] = jnp.where(low, v, v_sw).astype(BF16)
    vd_ref[1] = jnp.where(low, v_sw, v).astype(BF16)


def _layer_a(x2d, mod_a, a_norm, w_in, ln_g, ln_b, ws_m, bs_b, w_out, kv_norm, w_kv, cos_c, sin_c, batch, seq):
    n = x2d.shape[0]
    tiles = seq // ROW_TILE
    row = lambda b, i: (b * tiles + i, 0)
    return pl.pallas_call(
        _layer_a_kernel,
        grid=(batch, tiles),
        in_specs=[
            pl.BlockSpec((ROW_TILE, D_MODEL), row),
            pl.BlockSpec((1, 1, 3 * D_MODEL), lambda b, i: (b, 0, 0)),
            _const_spec((1, D_MODEL)),
            _const_spec((3, A_GROUPS, D_MODEL, A_GROUP_DIM)),
            _const_spec((A_GROUPS, 1, A_GROUP_DIM)),
            _const_spec((A_GROUPS, 1, A_GROUP_DIM)),
            _const_spec((A_GROUPS, CHUNK, CHUNK)),
            _const_spec((A_GROUPS, CHUNK, LANES)),
            _const_spec((A_GROUPS, A_GROUP_DIM, D_MODEL)),
            _const_spec((1, D_MODEL)),
            _const_spec((D_MODEL, 2 * LANES)),
            pl.BlockSpec((TABLE_ROWS, LANES), row),
            pl.BlockSpec((TABLE_ROWS, LANES), row),
        ],
        out_specs=[
            pl.BlockSpec((ROW_TILE, D_MODEL), row),
            pl.BlockSpec((KV_HEADS, ROW_TILE, LANES), lambda b, i: (0, b * tiles + i, 0)),
            pl.BlockSpec((KV_HEADS, ROW_TILE, LANES), lambda b, i: (0, b * tiles + i, 0)),
        ],
        out_shape=[
            jax.ShapeDtypeStruct((n, D_MODEL), F32),
            jax.ShapeDtypeStruct((KV_HEADS, n, LANES), BF16),
            jax.ShapeDtypeStruct((KV_HEADS, n, LANES), BF16),
        ],
        scratch_shapes=[
            pltpu.VMEM((ROW_TILE, D_MODEL), BF16),
            pltpu.VMEM((A_GROUPS, ROW_TILE, A_GROUP_DIM), F32),
            pltpu.VMEM((A_GROUPS, ROW_TILE, A_GROUP_DIM), BF16),
            pltpu.VMEM((ROW_TILE, LANES), F32),
            pltpu.VMEM((ROW_TILE, LANES), F32),
        ],
        compiler_params=pltpu.CompilerParams(
            dimension_semantics=("arbitrary", "arbitrary"), vmem_limit_bytes=VMEM_LIMIT_BYTES),
        name="layer_a",
    )(x2d, mod_a, a_norm, w_in, ln_g, ln_b, ws_m, bs_b, w_out, kv_norm, w_kv, cos_c, sin_c)


def _layer_b_kernel(sinks_ref, x_ref, mod_ref, bnorm_ref, wq_ref, wz_ref, kc_ref, kp_ref, vc_ref, vp_ref,
                    cosc_ref, sinc_ref, wout_ref, fnorm_ref, out_ref,
                    h_scr, q_scr, k_scr, v_scr, o_scr):
    tile = pl.program_id(1)
    x = x_ref[...]
    mod = mod_ref[0]
    shift = mod[:, 0:D_MODEL]
    scale = mod[:, D_MODEL:2 * D_MODEL]
    gate = mod[:, 2 * D_MODEL:3 * D_MODEL]
    rstd = lax.rsqrt(jnp.mean(x * x, axis=-1, keepdims=True) + EPS)
    gs = bnorm_ref[...] * (1.0 + scale)
    h_scr[...] = ((x * rstd) * gs + shift).astype(BF16)

    q = jnp.dot(h_scr[...], wq_ref[...], preferred_element_type=F32)
    cos_t, sin_t = _expand_tables(cosc_ref[...], sinc_ref[...])
    lane = lax.broadcasted_iota(jnp.int32, (ROW_TILE, LANES), 1)
    low = lane < HEAD_DIM
    qk_scale = 1.0 / math.sqrt(HEAD_DIM)
    for j in range(Q_HEADS // 2):
        qs = q[:, j * LANES:(j + 1) * LANES]
        qr = (qs * cos_t + _rotate_half_pairs(qs) * sin_t) * qk_scale
        q_scr[2 * j] = jnp.where(low, qr, 0.0).astype(BF16)
        q_scr[2 * j + 1] = jnp.where(low, 0.0, qr).astype(BF16)

    k_scr[:, 0:WINDOW, :] = kp_ref[...]
    k_scr[:, WINDOW:, :] = kc_ref[...]
    v_scr[:, 0:WINDOW, :] = vp_ref[...]
    v_scr[:, WINDOW:, :] = vc_ref[...]

    qi = lax.broadcasted_iota(jnp.int32, (WINDOW, 2 * WINDOW), 0)
    sj = lax.broadcasted_iota(jnp.int32, (WINDOW, 2 * WINDOW), 1)
    band = (sj > qi) & (sj <= qi + WINDOW)
    first_key = jnp.where(tile == 0, WINDOW, 0)
    olane = lax.broadcasted_iota(jnp.int32, (WINDOW, LANES), 1)
    olow = olane < HEAD_DIM
    dn = (((1,), (1,)), ((), ()))
    for r in range(ROW_TILE // WINDOW):
        rows = slice(r * WINDOW, (r + 1) * WINDOW)
        mask = (band & (sj >= first_key)) if r == 0 else band
        for kh in range(KV_HEADS):
            kb = k_scr[kh, r * WINDOW:(r + 2) * WINDOW, :]
            vb = v_scr[kh, r * WINDOW:(r + 2) * WINDOW, :]
            q8 = jnp.concatenate([q_scr[kh * Q_PER_KV + g, rows, :] for g in range(Q_PER_KV)], axis=0)
            logits = lax.dot_general(q8, kb, dn, preferred_element_type=F32)
            probs, rinv = [], []
            for g in range(Q_PER_KV):
                sink = sinks_ref[kh * Q_PER_KV + g]
                lg = jnp.where(mask, logits[g * WINDOW:(g + 1) * WINDOW], -1e30)
                m = jnp.maximum(jnp.max(lg, axis=-1, keepdims=True), sink)
                e = jnp.exp(lg - m)
                den = jnp.sum(e, axis=-1, keepdims=True) + jnp.exp(sink - m)
                probs.append(e.astype(BF16))
                rinv.append(1.0 / den)
            pv = jnp.dot(jnp.concatenate(probs, axis=0), vb, preferred_element_type=F32)
            for p in range(Q_PER_KV // 2):
                a = pv[(2 * p) * WINDOW:(2 * p + 1) * WINDOW] * rinv[2 * p]
                b = pv[(2 * p + 1) * WINDOW:(2 * p + 2) * WINDOW] * rinv[2 * p + 1]
                col = (kh * Q_PER_KV // 2 + p) * LANES
                o_scr[rows, col:col + LANES] = jnp.where(olow, a, b)

    z = jnp.dot(h_scr[...], wz_ref[...], preferred_element_type=F32)
    oz = (o_scr[...] * jax.nn.silu(z)).astype(BF16)
    x2 = x + gate * jnp.dot(oz, wout_ref[...], preferred_element_type=F32)
    rstd2 = lax.rsqrt(jnp.mean(x2 * x2, axis=-1, keepdims=True) + EPS)
    out_ref[...] = (x2 * rstd2) * fnorm_ref[...]


def _layer_b(x1, mod_b, b_norm, wq, wz, kd, vd, cos_c, sin_c, sinks, w_out, final_norm, batch, seq):
    n = x1.shape[0]
    tiles = seq // ROW_TILE
    blocks_per_tile = ROW_TILE // WINDOW
    row = lambda b, i: (b * tiles + i, 0)
    cur = lambda b, i: (0, b * tiles + i, 0)
    prev = lambda b, i: (0, jnp.maximum((b * tiles + i) * blocks_per_tile - 1, 0), 0)
    return pl.pallas_call(
        _layer_b_kernel,
        grid=(batch, tiles),
        in_specs=[
            pl.BlockSpec(memory_space=pltpu.SMEM),
            pl.BlockSpec((ROW_TILE, D_MODEL), row),
            pl.BlockSpec((1, 1, 3 * D_MODEL), lambda b, i: (b, 0, 0)),
            _const_spec((1, D_MODEL)),
            _const_spec((D_MODEL, D_MODEL)),
            _const_spec((D_MODEL, D_MODEL)),
            pl.BlockSpec((KV_HEADS, ROW_TILE, LANES), cur),
            pl.BlockSpec((KV_HEADS, WINDOW, LANES), prev),
            pl.BlockSpec((KV_HEADS, ROW_TILE, LANES), cur),
            pl.BlockSpec((KV_HEADS, WINDOW, LANES), prev),
            pl.BlockSpec((TABLE_ROWS, LANES), row),
            pl.BlockSpec((TABLE_ROWS, LANES), row),
            _const_spec((D_MODEL, D_MODEL)),
            _const_spec((1, D_MODEL)),
        ],
        out_specs=pl.BlockSpec((ROW_TILE, D_MODEL), row),
        out_shape=jax.ShapeDtypeStruct((n, D_MODEL), F32),
        scratch_shapes=[
            pltpu.VMEM((ROW_TILE, D_MODEL), BF16),
            pltpu.VMEM((Q_HEADS, ROW_TILE, LANES), BF16),
            pltpu.VMEM((KV_HEADS, ROW_TILE + WINDOW, LANES), BF16),
            pltpu.VMEM((KV_HEADS, ROW_TILE + WINDOW, LANES), BF16),
            pltpu.VMEM((ROW_TILE, D_MODEL), F32),
        ],
        compiler_params=pltpu.CompilerParams(
            dimension_semantics=("arbitrary", "arbitrary"), vmem_limit_bytes=VMEM_LIMIT_BYTES),
        name="layer_b",
    )(sinks, x1, mod_b, b_norm, wq, wz, kd, kd, vd, vd, cos_c, sin_c, w_out, final_norm)


def kernel(x, c, positions, a_norm, a_ada_w, a_ada_b, a_w_in, a_ln_g, a_ln_b, a_w_s, a_b_s, a_w_out,
           kv_norm, w_kv, b_norm, b_ada_w, b_ada_b, b_w_in, b_sinks, b_w_out, final_norm):
    batch, seq, d = x.shape
    assert d == D_MODEL and seq % ROW_TILE == 0
    assert a_norm.shape[0] == 1 and b_norm.shape[0] == 1, "one layer of each mixer"
    n = batch * seq

    mod_a, mod_b, ws_m = _prep(c, a_ada_w, a_ada_b, b_ada_w, b_ada_b, a_w_s)
    cos_c, sin_c = _rope_tables(positions)

    w_in = a_w_in[0].reshape(D_MODEL, 3, A_GROUPS, A_GROUP_DIM).transpose(1, 2, 0, 3).astype(BF16)
    w_out_a = a_w_out[0].reshape(A_GROUPS, A_GROUP_DIM, D_MODEL).astype(BF16)
    ln_g = a_ln_g[0].reshape(A_GROUPS, 1, A_GROUP_DIM)
    ln_b = a_ln_b[0].reshape(A_GROUPS, 1, A_GROUP_DIM)
    bs_b = jnp.broadcast_to(a_b_s[0][:, :, None], (A_GROUPS, CHUNK, LANES))
    wq = b_w_in[0][:, :D_MODEL].astype(BF16)
    wz = b_w_in[0][:, D_MODEL:].astype(BF16)

    x1, kd, vd = _layer_a(
        x.reshape(n, D_MODEL), mod_a.reshape(batch, 1, 3 * D_MODEL), a_norm, w_in, ln_g, ln_b, ws_m, bs_b,
        w_out_a, kv_norm.reshape(1, D_MODEL), w_kv.astype(BF16), cos_c, sin_c, batch, seq)
    out = _layer_b(
        x1, mod_b.reshape(batch, 1, 3 * D_MODEL), b_norm, wq, wz, kd, vd, cos_c, sin_c, b_sinks[0],
        b_w_out[0].astype(BF16), final_norm.reshape(1, D_MODEL), batch, seq)
    return out.reshape(batch, seq, D_MODEL)
```

```python
import functools
import math

import jax
import jax.numpy as jnp
from jax import lax
from jax.experimental import pallas as pl
from jax.experimental.pallas import tpu as pltpu

F32 = jnp.float32
BF16 = jnp.bfloat16

D_MODEL = 1024
A_WIDTH = 2048
A_GROUPS = 8
A_GROUP_DIM = A_WIDTH // A_GROUPS
CHUNK = 128
HEAD_DIM = 64
HALF_DIM = HEAD_DIM // 2
Q_HEADS = 16
KV_HEADS = 2
Q_PER_KV = Q_HEADS // KV_HEADS
WINDOW = 128
ROPE_THETA = 10000.0
EPS = 1e-6
LOG2_E = math.log2(math.e)

LANES = 128
ROW_TILE = 512
POS_PER_ROW = LANES // HALF_DIM
TABLE_ROWS = ROW_TILE // POS_PER_ROW
VMEM_LIMIT_BYTES = 56 * 1024 * 1024


def _const_spec(shape):
    zeros = (0,) * len(shape)
    return pl.BlockSpec(shape, lambda *_: zeros, pipeline_mode=pl.Buffered(1))


def _prep_kernel(c_ref, wa_ref, ba_ref, wb_ref, bb_ref, ws_ref, moda_ref, modb_ref, wsm_ref):
    sc = jax.nn.silu(c_ref[...])
    moda_ref[...] = jnp.dot(sc, wa_ref[0], preferred_element_type=F32,
                            precision=lax.Precision.HIGHEST) + ba_ref[0]
    modb_ref[...] = jnp.dot(sc, wb_ref[0], preferred_element_type=F32,
                            precision=lax.Precision.HIGHEST) + bb_ref[0]
    t = lax.broadcasted_iota(jnp.int32, (CHUNK, CHUNK), 0)
    s = lax.broadcasted_iota(jnp.int32, (CHUNK, CHUNK), 1)
    causal = (s <= t).astype(F32)
    for g in range(A_GROUPS):
        wsm_ref[g] = (ws_ref[0, g] * causal).astype(BF16)


def _prep(c, a_ada_w, a_ada_b, b_ada_w, b_ada_b, a_w_s):
    batch = c.shape[0]
    col = 1024
    n_col = 3 * D_MODEL // col
    return pl.pallas_call(
        _prep_kernel,
        grid=(n_col,),
        in_specs=[
            pl.BlockSpec((batch, D_MODEL), lambda j: (0, 0)),
            pl.BlockSpec((1, D_MODEL, col), lambda j: (0, 0, j)),
            pl.BlockSpec((1, 1, col), lambda j: (0, 0, j)),
            pl.BlockSpec((1, D_MODEL, col), lambda j: (0, 0, j)),
            pl.BlockSpec((1, 1, col), lambda j: (0, 0, j)),
            pl.BlockSpec((1, A_GROUPS, CHUNK, CHUNK), lambda j: (0, 0, 0, 0)),
        ],
        out_specs=[
            pl.BlockSpec((batch, col), lambda j: (0, j)),
            pl.BlockSpec((batch, col), lambda j: (0, j)),
            pl.BlockSpec((A_GROUPS, CHUNK, CHUNK), lambda j: (0, 0, 0)),
        ],
        out_shape=[
            jax.ShapeDtypeStruct((batch, 3 * D_MODEL), F32),
            jax.ShapeDtypeStruct((batch, 3 * D_MODEL), F32),
            jax.ShapeDtypeStruct((A_GROUPS, CHUNK, CHUNK), BF16),
        ],
        compiler_params=pltpu.CompilerParams(dimension_semantics=("arbitrary",)),
        name="prep",
    )(c, a_ada_w, a_ada_b.reshape(1, 1, -1), b_ada_w, b_ada_b.reshape(1, 1, -1), a_w_s)


def _rope_kernel(pos_ref, invf_ref, cos_ref, sin_ref):
    ang = pos_ref[...].astype(F32) * invf_ref[...]
    cos_ref[...] = jnp.cos(ang)
    sin_ref[...] = jnp.sin(ang)


def _rope_tables(positions):
    n = positions.size
    inv_freq = ROPE_THETA ** (-jnp.arange(0, HEAD_DIM, 2, dtype=F32) / HEAD_DIM)
    invf = jnp.tile(inv_freq, POS_PER_ROW).reshape(1, LANES)
    pos = positions.reshape(n // ROW_TILE, POS_PER_ROW, TABLE_ROWS)
    pos = jnp.repeat(jnp.swapaxes(pos, 1, 2), HALF_DIM, axis=2).reshape(n // POS_PER_ROW, LANES)
    rows = 2048
    return pl.pallas_call(
        _rope_kernel,
        grid=(pos.shape[0] // rows,),
        in_specs=[pl.BlockSpec((rows, LANES), lambda i: (i, 0)),
                  pl.BlockSpec((1, LANES), lambda i: (0, 0))],
        out_specs=[pl.BlockSpec((rows, LANES), lambda i: (i, 0)),
                   pl.BlockSpec((rows, LANES), lambda i: (i, 0))],
        out_shape=[jax.ShapeDtypeStruct(pos.shape, F32), jax.ShapeDtypeStruct(pos.shape, F32)],
        compiler_params=pltpu.CompilerParams(dimension_semantics=("arbitrary",)),
        name="rope_tables",
    )(pos, invf)


def _expand_tables(cos_c, sin_c):
    lane = lax.broadcasted_iota(jnp.int32, (TABLE_ROWS, LANES), 1)
    first_half = (lane & (HEAD_DIM - 1)) < HALF_DIM

    def spread(t):
        t = jnp.where(lane < HALF_DIM, t, pltpu.roll(t, HALF_DIM, 1))
        return jnp.where(lane < HEAD_DIM, t, pltpu.roll(t, HEAD_DIM, 1))

    cos_parts, sin_parts = [], []
    for a in range(POS_PER_ROW):
        shift = (LANES - HALF_DIM * a) % LANES
        c = cos_c if shift == 0 else pltpu.roll(cos_c, shift, 1)
        s = sin_c if shift == 0 else pltpu.roll(sin_c, shift, 1)
        cos_parts.append(spread(c))
        s = spread(s)
        sin_parts.append(jnp.where(first_half, -s, s))
    return jnp.concatenate(cos_parts, axis=0), jnp.concatenate(sin_parts, axis=0)


def _rotate_half_pairs(x):
    lane = lax.broadcasted_iota(jnp.int32, x.shape, 1)
    first_half = (lane & (HEAD_DIM - 1)) < HALF_DIM
    return jnp.where(first_half, pltpu.roll(x, LANES - HALF_DIM, 1), pltpu.roll(x, HALF_DIM, 1))


def _layer_a_kernel(x_ref, mod_ref, anorm_ref, win_ref, lng_ref, lnb_ref, ws_ref, bs_ref, wout_ref,
                    kvn_ref, wkv_ref, cosc_ref, sinc_ref,
                    x1_ref, kd_ref, vd_ref,
                    h_scr, gv_scr, y_scr, s1_scr, s2_scr):
    x = x_ref[...]
    mod = mod_ref[0]
    shift = mod[:, 0:D_MODEL]
    scale = mod[:, D_MODEL:2 * D_MODEL]
    gate = mod[:, 2 * D_MODEL:3 * D_MODEL]
    rstd = lax.rsqrt(jnp.mean(x * x, axis=-1, keepdims=True) + EPS)
    gs = anorm_ref[...] * (1.0 + scale)
    h_scr[...] = ((x * rstd) * gs + shift).astype(BF16)

    s1 = s2 = None
    for g in range(A_GROUPS):
        v = jnp.dot(h_scr[...], win_ref[1, g], preferred_element_type=F32)
        gv = jax.nn.gelu(v)
        gv_scr[g] = gv
        lo = gv[:, :LANES]
        hi = gv[:, LANES:]
        s1 = lo + hi if s1 is None else s1 + (lo + hi)
        s2 = lo * lo + hi * hi if s2 is None else s2 + (lo * lo + hi * hi)
    mu = jnp.sum(s1, axis=-1, keepdims=True) * (1.0 / A_WIDTH)
    var = jnp.sum(s2, axis=-1, keepdims=True) * (1.0 / A_WIDTH) - mu * mu
    s1_scr[...] = jnp.broadcast_to(mu, s1_scr.shape)
    s2_scr[...] = jnp.broadcast_to(lax.rsqrt(var + EPS), s2_scr.shape)

    for g in range(A_GROUPS):
        mu_b = s1_scr[...]
        rs_b = s2_scr[...]
        gv = gv_scr[g]
        lg = lng_ref[g]
        lb = lnb_ref[g]
        vn = jnp.concatenate(
            [((gv[:, :LANES] - mu_b) * rs_b) * lg[:, :LANES] + lb[:, :LANES],
             ((gv[:, LANES:] - mu_b) * rs_b) * lg[:, LANES:] + lb[:, LANES:]], axis=1).astype(BF16)
        h = h_scr[...]
        u = jnp.dot(h, win_ref[0, g], preferred_element_type=F32)
        z = jnp.dot(h, win_ref[2, g], preferred_element_type=F32)
        ws = ws_ref[g]
        bs = bs_ref[g]
        bs2 = jnp.concatenate([bs, bs], axis=1)
        for c in range(ROW_TILE // CHUNK):
            rows = slice(c * CHUNK, (c + 1) * CHUNK)
            mixed = jnp.dot(ws, vn[rows], preferred_element_type=F32) + bs2
            y = jax.nn.gelu(u[rows]) * mixed * jax.nn.silu(z[rows])
            y_scr[rows, g * A_GROUP_DIM:(g + 1) * A_GROUP_DIM] = y.astype(BF16)

    out = jnp.dot(y_scr[...], wout_ref[...], preferred_element_type=F32)
    x1 = x + gate * out
    x1_ref[...] = x1

    rstd2 = lax.rsqrt(jnp.mean(x1 * x1, axis=-1, keepdims=True) + EPS)
    hk = ((x1 * rstd2) * kvn_ref[...]).astype(BF16)
    kv = jnp.dot(hk, wkv_ref[...], preferred_element_type=F32)
    k = kv[:, :LANES]
    v = kv[:, LANES:]
    cos_t, sin_t = _expand_tables(cosc_ref[...], sinc_ref[...])
    k = k * cos_t + _rotate_half_pairs(k) * sin_t
    lane = lax.broadcasted_iota(jnp.int32, k.shape, 1)
    low = lane < HEAD_DIM
    k_sw = pltpu.roll(k, HEAD_DIM, 1)
    v_sw = pltpu.roll(v, HEAD_DIM, 1)
    kd_ref[0] = jnp.where(low, k, k_sw).astype(BF16)
    kd_ref[1] = jnp.where(low, k_sw, k).astype(BF16)
    vd_ref[0] = jnp.where(low, v, v_sw).astype(BF16)
    vd_ref[1] = jnp.where(low, v_sw, v).astype(BF16)


def _layer_a(x2d, mod_a, a_norm, w_in, ln_g, ln_b, ws_m, bs_b, w_out, kv_norm, w_kv, cos_c, sin_c, batch, seq):
    n = x2d.shape[0]
    tiles = seq // ROW_TILE
    row = lambda b, i: (b * tiles + i, 0)
    return pl.pallas_call(
        _layer_a_kernel,
        grid=(batch, tiles),
        in_specs=[
            pl.BlockSpec((ROW_TILE, D_MODEL), row),
            pl.BlockSpec((1, 1, 3 * D_MODEL), lambda b, i: (b, 0, 0)),
            _const_spec((1, D_MODEL)),
            _const_spec((3, A_GROUPS, D_MODEL, A_GROUP_DIM)),
            _const_spec((A_GROUPS, 1, A_GROUP_DIM)),
            _const_spec((A_GROUPS, 1, A_GROUP_DIM)),
            _const_spec((A_GROUPS, CHUNK, CHUNK)),
            _const_spec((A_GROUPS, CHUNK, LANES)),
            _const_spec((A_WIDTH, D_MODEL)),
            _const_spec((1, D_MODEL)),
            _const_spec((D_MODEL, 2 * LANES)),
            pl.BlockSpec((TABLE_ROWS, LANES), row),
            pl.BlockSpec((TABLE_ROWS, LANES), row),
        ],
        out_specs=[
            pl.BlockSpec((ROW_TILE, D_MODEL), row),
            pl.BlockSpec((KV_HEADS, ROW_TILE, LANES), lambda b, i: (0, b * tiles + i, 0)),
            pl.BlockSpec((KV_HEADS, ROW_TILE, LANES), lambda b, i: (0, b * tiles + i, 0)),
        ],
        out_shape=[
            jax.ShapeDtypeStruct((n, D_MODEL), F32),
            jax.ShapeDtypeStruct((KV_HEADS, n, LANES), BF16),
            jax.ShapeDtypeStruct((KV_HEADS, n, LANES), BF16),
        ],
        scratch_shapes=[
            pltpu.VMEM((ROW_TILE, D_MODEL), BF16),
            pltpu.VMEM((A_GROUPS, ROW_TILE, A_GROUP_DIM), F32),
            pltpu.VMEM((ROW_TILE, A_WIDTH), BF16),
            pltpu.VMEM((ROW_TILE, LANES), F32),
            pltpu.VMEM((ROW_TILE, LANES), F32),
        ],
        compiler_params=pltpu.CompilerParams(
            dimension_semantics=("arbitrary", "arbitrary"), vmem_limit_bytes=VMEM_LIMIT_BYTES),
        name="layer_a",
    )(x2d, mod_a, a_norm, w_in, ln_g, ln_b, ws_m, bs_b, w_out, kv_norm, w_kv, cos_c, sin_c)


def _layer_b_kernel(sinks_ref, x_ref, mod_ref, bnorm_ref, wq_ref, wz_ref, kc_ref, kp_ref, vc_ref, vp_ref,
                    cosc_ref, sinc_ref, wout_ref, fnorm_ref, out_ref,
                    h_scr, q_scr, k_scr, v_scr, o_scr):
    tile = pl.program_id(1)
    x = x_ref[...]
    mod = mod_ref[0]
    shift = mod[:, 0:D_MODEL]
    scale = mod[:, D_MODEL:2 * D_MODEL]
    gate = mod[:, 2 * D_MODEL:3 * D_MODEL]
    rstd = lax.rsqrt(jnp.mean(x * x, axis=-1, keepdims=True) + EPS)
    gs = bnorm_ref[...] * (1.0 + scale)
    h_scr[...] = ((x * rstd) * gs + shift).astype(BF16)

    q = jnp.dot(h_scr[...], wq_ref[...], preferred_element_type=F32)
    cos_t, sin_t = _expand_tables(cosc_ref[...], sinc_ref[...])
    lane = lax.broadcasted_iota(jnp.int32, (ROW_TILE, LANES), 1)
    low = lane < HEAD_DIM
    qk_scale = LOG2_E / math.sqrt(HEAD_DIM)
    for j in range(Q_HEADS // 2):
        qs = q[:, j * LANES:(j + 1) * LANES]
        qr = (qs * cos_t + _rotate_half_pairs(qs) * sin_t) * qk_scale
        q_scr[2 * j] = jnp.where(low, qr, 0.0).astype(BF16)
        q_scr[2 * j + 1] = jnp.where(low, 0.0, qr).astype(BF16)

    k_scr[:, 0:WINDOW, :] = kp_ref[...]
    k_scr[:, WINDOW:, :] = kc_ref[...]
    v_scr[:, 0:WINDOW, :] = vp_ref[...]
    v_scr[:, WINDOW:, :] = vc_ref[...]

    qi = lax.broadcasted_iota(jnp.int32, (WINDOW, WINDOW), 0)
    sj = lax.broadcasted_iota(jnp.int32, (WINDOW, WINDOW), 1)
    from_prev = sj > qi
    olow = lax.broadcasted_iota(jnp.int32, (WINDOW, LANES), 1) < HEAD_DIM
    dn = (((1,), (1,)), ((), ()))

    def attend(row0, prev_penalty):
        rows = pl.ds(row0, WINDOW)
        for kh in range(KV_HEADS):
            kb = k_scr[kh, pl.ds(row0, 2 * WINDOW), :]
            vb = v_scr[kh, pl.ds(row0, 2 * WINDOW), :]
            q8 = jnp.concatenate([q_scr[kh * Q_PER_KV + g, rows, :] for g in range(Q_PER_KV)], axis=0)
            logits = lax.dot_general(q8, kb, dn, preferred_element_type=F32)
            probs, rinv = [], []
            for g in range(Q_PER_KV):
                sink = sinks_ref[kh * Q_PER_KV + g] * LOG2_E
                lg_prev = logits[g * WINDOW:(g + 1) * WINDOW, :WINDOW]
                if prev_penalty is not None:
                    lg_prev = lg_prev + prev_penalty
                lg = jnp.where(from_prev, lg_prev, logits[g * WINDOW:(g + 1) * WINDOW, WINDOW:])
                m = jnp.maximum(jnp.max(lg, axis=-1, keepdims=True), sink)
                e = jnp.exp2(lg - m)
                den = jnp.sum(e, axis=-1, keepdims=True) + jnp.exp2(sink - m)
                probs.append(jnp.concatenate(
                    [jnp.where(from_prev, e, 0.0), jnp.where(from_prev, 0.0, e)], axis=1).astype(BF16))
                rinv.append(1.0 / den)
            pv = jnp.dot(jnp.concatenate(probs, axis=0), vb, preferred_element_type=F32)
            for p in range(Q_PER_KV // 2):
                a = pv[(2 * p) * WINDOW:(2 * p + 1) * WINDOW] * rinv[2 * p]
                b = pv[(2 * p + 1) * WINDOW:(2 * p + 2) * WINDOW] * rinv[2 * p + 1]
                col = (kh * Q_PER_KV // 2 + p) * LANES
                o_scr[rows, col:col + LANES] = jnp.where(olow, a, b)

    attend(0, jnp.where(tile == 0, -1e30, 0.0))

    def attend_body(r, carry):
        attend(pl.multiple_of(r * WINDOW, WINDOW), None)
        return carry

    lax.fori_loop(1, ROW_TILE // WINDOW, attend_body, 0)

    z = jnp.dot(h_scr[...], wz_ref[...], preferred_element_type=F32)
    oz = (o_scr[...] * jax.nn.silu(z)).astype(BF16)
    x2 = x + gate * jnp.dot(oz, wout_ref[...], preferred_element_type=F32)
    rstd2 = lax.rsqrt(jnp.mean(x2 * x2, axis=-1, keepdims=True) + EPS)
    out_ref[...] = (x2 * rstd2) * fnorm_ref[...]


def _layer_b(x1, mod_b, b_norm, wq, wz, kd, vd, cos_c, sin_c, sinks, w_out, final_norm, batch, seq):
    n = x1.shape[0]
    tiles = seq // ROW_TILE
    blocks_per_tile = ROW_TILE // WINDOW
    row = lambda b, i: (b * tiles + i, 0)
    cur = lambda b, i: (0, b * tiles + i, 0)
    prev = lambda b, i: (0, jnp.maximum((b * tiles + i) * blocks_per_tile - 1, 0), 0)
    return pl.pallas_call(
        _layer_b_kernel,
        grid=(batch, tiles),
        in_specs=[
            pl.BlockSpec(memory_space=pltpu.SMEM),
            pl.BlockSpec((ROW_TILE, D_MODEL), row),
            pl.BlockSpec((1, 1, 3 * D_MODEL), lambda b, i: (b, 0, 0)),
            _const_spec((1, D_MODEL)),
            _const_spec((D_MODEL, D_MODEL)),
            _const_spec((D_MODEL, D_MODEL)),
            pl.BlockSpec((KV_HEADS, ROW_TILE, LANES), cur),
            pl.BlockSpec((KV_HEADS, WINDOW, LANES), prev),
            pl.BlockSpec((KV_HEADS, ROW_TILE, LANES), cur),
            pl.BlockSpec((KV_HEADS, WINDOW, LANES), prev),
            pl.BlockSpec((TABLE_ROWS, LANES), row),
            pl.BlockSpec((TABLE_ROWS, LANES), row),
            _const_spec((D_MODEL, D_MODEL)),
            _const_spec((1, D_MODEL)),
        ],
        out_specs=pl.BlockSpec((ROW_TILE, D_MODEL), row),
        out_shape=jax.ShapeDtypeStruct((n, D_MODEL), F32),
        scratch_shapes=[
            pltpu.VMEM((ROW_TILE, D_MODEL), BF16),
            pltpu.VMEM((Q_HEADS, ROW_TILE, LANES), BF16),
            pltpu.VMEM((KV_HEADS, ROW_TILE + WINDOW, LANES), BF16),
            pltpu.VMEM((KV_HEADS, ROW_TILE + WINDOW, LANES), BF16),
            pltpu.VMEM((ROW_TILE, D_MODEL), F32),
        ],
        compiler_params=pltpu.CompilerParams(
            dimension_semantics=("arbitrary", "arbitrary"), vmem_limit_bytes=VMEM_LIMIT_BYTES),
        name="layer_b",
    )(sinks, x1, mod_b, b_norm, wq, wz, kd, kd, vd, vd, cos_c, sin_c, w_out, final_norm)


def kernel(x, c, positions, a_norm, a_ada_w, a_ada_b, a_w_in, a_ln_g, a_ln_b, a_w_s, a_b_s, a_w_out,
           kv_norm, w_kv, b_norm, b_ada_w, b_ada_b, b_w_in, b_sinks, b_w_out, final_norm):
    batch, seq, d = x.shape
    assert d == D_MODEL and seq % ROW_TILE == 0
    assert a_norm.shape[0] == 1 and b_norm.shape[0] == 1, "one layer of each mixer"
    n = batch * seq

    mod_a, mod_b, ws_m = _prep(c, a_ada_w, a_ada_b, b_ada_w, b_ada_b, a_w_s)
    cos_c, sin_c = _rope_tables(positions)

    w_in = a_w_in[0].reshape(D_MODEL, 3, A_GROUPS, A_GROUP_DIM).transpose(1, 2, 0, 3).astype(BF16)
    w_out_a = a_w_out[0].astype(BF16)
    ln_g = a_ln_g[0].reshape(A_GROUPS, 1, A_GROUP_DIM)
    ln_b = a_ln_b[0].reshape(A_GROUPS, 1, A_GROUP_DIM)
    bs_b = jnp.broadcast_to(a_b_s[0][:, :, None], (A_GROUPS, CHUNK, LANES))
    wq = b_w_in[0][:, :D_MODEL].astype(BF16)
    wz = b_w_in[0][:, D_MODEL:].astype(BF16)

    x1, kd, vd = _layer_a(
        x.reshape(n, D_MODEL), mod_a.reshape(batch, 1, 3 * D_MODEL), a_norm, w_in, ln_g, ln_b, ws_m, bs_b,
        w_out_a, kv_norm.reshape(1, D_MODEL), w_kv.astype(BF16), cos_c, sin_c, batch, seq)
    out = _layer_b(
        x1, mod_b.reshape(batch, 1, 3 * D_MODEL), b_norm, wq, wz, kd, vd, cos_c, sin_c, b_sinks[0],
        b_w_out[0].astype(BF16), final_norm.reshape(1, D_MODEL), batch, seq)
    return out.reshape(batch, seq, D_MODEL)
```

```python
import functools
import math

import jax
import jax.numpy as jnp
import numpy as np
from jax import lax
from jax.experimental import pallas as pl
from jax.experimental.pallas import tpu as pltpu

F32 = jnp.float32
BF16 = jnp.bfloat16

D_MODEL = 1024
A_WIDTH = 2048
A_GROUPS = 8
A_GROUP_DIM = A_WIDTH // A_GROUPS
CHUNK = 128
HEAD_DIM = 64
HALF_DIM = HEAD_DIM // 2
Q_HEADS = 16
KV_HEADS = 2
Q_PER_KV = Q_HEADS // KV_HEADS
WINDOW = 128
ROPE_THETA = 10000.0
EPS = 1e-6
LOG2_E = math.log2(math.e)

LANES = 128
ROW_TILE = 512
POS_PER_ROW = LANES // HALF_DIM
TABLE_ROWS = ROW_TILE // POS_PER_ROW
B_SUBTILES = 2
B_ROWS = B_SUBTILES * ROW_TILE
B_COL_CHUNK = 256
B_COL_CHUNKS = D_MODEL // B_COL_CHUNK
B_CHUNK_HEADS = B_COL_CHUNK // HEAD_DIM
VMEM_LIMIT_BYTES = 60 * 1024 * 1024


def _const_spec(shape):
    zeros = (0,) * len(shape)
    return pl.BlockSpec(shape, lambda *_: zeros, pipeline_mode=pl.Buffered(1))


def _prep_kernel(c_ref, wa_ref, ba_ref, wb_ref, bb_ref, ws_ref, moda_ref, modb_ref, wsm_ref):
    sc = jax.nn.silu(c_ref[...])
    moda_ref[...] = jnp.dot(sc, wa_ref[0], preferred_element_type=F32,
                            precision=lax.Precision.HIGHEST) + ba_ref[0]
    modb_ref[...] = jnp.dot(sc, wb_ref[0], preferred_element_type=F32,
                            precision=lax.Precision.HIGHEST) + bb_ref[0]
    t = lax.broadcasted_iota(jnp.int32, (CHUNK, CHUNK), 0)
    s = lax.broadcasted_iota(jnp.int32, (CHUNK, CHUNK), 1)
    causal = (s <= t).astype(F32)
    for g in range(A_GROUPS):
        wsm_ref[g] = (ws_ref[0, g] * causal).astype(BF16)


def _prep(c, a_ada_w, a_ada_b, b_ada_w, b_ada_b, a_w_s):
    batch = c.shape[0]
    col = 1024
    n_col = 3 * D_MODEL // col
    return pl.pallas_call(
        _prep_kernel,
        grid=(n_col,),
        in_specs=[
            pl.BlockSpec((batch, D_MODEL), lambda j: (0, 0)),
            pl.BlockSpec((1, D_MODEL, col), lambda j: (0, 0, j)),
            pl.BlockSpec((1, 1, col), lambda j: (0, 0, j)),
            pl.BlockSpec((1, D_MODEL, col), lambda j: (0, 0, j)),
            pl.BlockSpec((1, 1, col), lambda j: (0, 0, j)),
            pl.BlockSpec((1, A_GROUPS, CHUNK, CHUNK), lambda j: (0, 0, 0, 0)),
        ],
        out_specs=[
            pl.BlockSpec((batch, col), lambda j: (0, j)),
            pl.BlockSpec((batch, col), lambda j: (0, j)),
            pl.BlockSpec((A_GROUPS, CHUNK, CHUNK), lambda j: (0, 0, 0)),
        ],
        out_shape=[
            jax.ShapeDtypeStruct((batch, 3 * D_MODEL), F32),
            jax.ShapeDtypeStruct((batch, 3 * D_MODEL), F32),
            jax.ShapeDtypeStruct((A_GROUPS, CHUNK, CHUNK), BF16),
        ],
        compiler_params=pltpu.CompilerParams(dimension_semantics=("arbitrary",)),
        name="prep",
    )(c, a_ada_w, a_ada_b.reshape(1, 1, -1), b_ada_w, b_ada_b.reshape(1, 1, -1), a_w_s)


def _rope_kernel(pos_ref, invf_ref, cos_ref, sin_ref):
    ang = pos_ref[...].astype(F32) * invf_ref[...]
    cos_ref[...] = jnp.cos(ang)
    sin_ref[...] = jnp.sin(ang)


def _rope_tables(positions):
    n = positions.size
    inv_freq = ROPE_THETA ** (-jnp.arange(0, HEAD_DIM, 2, dtype=F32) / HEAD_DIM)
    invf = jnp.tile(inv_freq, POS_PER_ROW).reshape(1, LANES)
    pos = positions.reshape(n // ROW_TILE, POS_PER_ROW, TABLE_ROWS)
    pos = jnp.repeat(jnp.swapaxes(pos, 1, 2), HALF_DIM, axis=2).reshape(n // POS_PER_ROW, LANES)
    rows = 2048
    return pl.pallas_call(
        _rope_kernel,
        grid=(pos.shape[0] // rows,),
        in_specs=[pl.BlockSpec((rows, LANES), lambda i: (i, 0)),
                  pl.BlockSpec((1, LANES), lambda i: (0, 0))],
        out_specs=[pl.BlockSpec((rows, LANES), lambda i: (i, 0)),
                   pl.BlockSpec((rows, LANES), lambda i: (i, 0))],
        out_shape=[jax.ShapeDtypeStruct(pos.shape, F32), jax.ShapeDtypeStruct(pos.shape, F32)],
        compiler_params=pltpu.CompilerParams(dimension_semantics=("arbitrary",)),
        name="rope_tables",
    )(pos, invf)


def _expand_tables(cos_c, sin_c):
    lane = lax.broadcasted_iota(jnp.int32, (TABLE_ROWS, LANES), 1)

    def spread(t):
        t = jnp.where(lane < HALF_DIM, t, pltpu.roll(t, HALF_DIM, 1))
        return jnp.where(lane < HEAD_DIM, t, pltpu.roll(t, HEAD_DIM, 1))

    cos_parts, sin_parts = [], []
    for a in range(POS_PER_ROW):
        shift = (LANES - HALF_DIM * a) % LANES
        cos_parts.append(spread(cos_c if shift == 0 else pltpu.roll(cos_c, shift, 1)))
        sin_parts.append(spread(sin_c if shift == 0 else pltpu.roll(sin_c, shift, 1)))
    return jnp.concatenate(cos_parts, axis=0), jnp.concatenate(sin_parts, axis=0)


def _lane_copy_matrix(src_lane_of_out):
    src = np.asarray(src_lane_of_out)
    e = np.zeros((LANES, src.size), np.float32)
    e[src, np.arange(src.size)] = 1.0
    return jnp.asarray(e, dtype=BF16)


def _layer_a_kernel(x_ref, mod_ref, anorm_ref, win_ref, lng_ref, lnb_ref, ws_ref, bs_ref, wout_ref,
                    kvn_ref, wkv_ref, kcopy_ref, vcopy_ref, cosc_ref, sinc_ref,
                    x1_ref, kd_ref, vd_ref,
                    h_scr, gv_scr, y_scr, s1_scr, s2_scr):
    x = x_ref[...]
    mod = mod_ref[0]
    shift = mod[:, 0:D_MODEL]
    scale = mod[:, D_MODEL:2 * D_MODEL]
    gate = mod[:, 2 * D_MODEL:3 * D_MODEL]
    rstd = lax.rsqrt(jnp.mean(x * x, axis=-1, keepdims=True) + EPS)
    gs = anorm_ref[...] * (1.0 + scale)
    h_scr[...] = ((x * rstd) * gs + shift).astype(BF16)

    s1 = s2 = None
    for g in range(A_GROUPS):
        v = jnp.dot(h_scr[...], win_ref[1, g], preferred_element_type=F32)
        gv = jax.nn.gelu(v)
        gv_scr[g] = gv
        lo = gv[:, :LANES]
        hi = gv[:, LANES:]
        s1 = lo + hi if s1 is None else s1 + (lo + hi)
        s2 = lo * lo + hi * hi if s2 is None else s2 + (lo * lo + hi * hi)
    mu = jnp.sum(s1, axis=-1, keepdims=True) * (1.0 / A_WIDTH)
    var = jnp.sum(s2, axis=-1, keepdims=True) * (1.0 / A_WIDTH) - mu * mu
    s1_scr[...] = jnp.broadcast_to(mu, s1_scr.shape)
    s2_scr[...] = jnp.broadcast_to(lax.rsqrt(var + EPS), s2_scr.shape)

    for g in range(A_GROUPS):
        mu_b = s1_scr[...]
        rs_b = s2_scr[...]
        gv = gv_scr[g]
        lg = lng_ref[g]
        lb = lnb_ref[g]
        vn = jnp.concatenate(
            [((gv[:, :LANES] - mu_b) * rs_b) * lg[:, :LANES] + lb[:, :LANES],
             ((gv[:, LANES:] - mu_b) * rs_b) * lg[:, LANES:] + lb[:, LANES:]], axis=1).astype(BF16)
        h = h_scr[...]
        u = jnp.dot(h, win_ref[0, g], preferred_element_type=F32)
        z = jnp.dot(h, win_ref[2, g], preferred_element_type=F32)
        ws = ws_ref[g]
        bs = bs_ref[g]
        bs2 = jnp.concatenate([bs, bs], axis=1)
        for c in range(ROW_TILE // CHUNK):
            rows = slice(c * CHUNK, (c + 1) * CHUNK)
            mixed = jnp.dot(ws, vn[rows], preferred_element_type=F32) + bs2
            y = jax.nn.gelu(u[rows]) * mixed * jax.nn.silu(z[rows])
            y_scr[rows, g * A_GROUP_DIM:(g + 1) * A_GROUP_DIM] = y.astype(BF16)

    out = jnp.dot(y_scr[...], wout_ref[...], preferred_element_type=F32)
    x1 = x + gate * out
    x1_ref[...] = x1

    rstd2 = lax.rsqrt(jnp.mean(x1 * x1, axis=-1, keepdims=True) + EPS)
    hk = ((x1 * rstd2) * kvn_ref[...]).astype(BF16)
    kv = jnp.dot(hk, wkv_ref[...], preferred_element_type=F32)
    k = kv[:, :LANES]
    v = kv[:, LANES:]
    cos_t, sin_t = _expand_tables(cosc_ref[...], sinc_ref[...])
    lane = lax.broadcasted_iota(jnp.int32, k.shape, 1)
    k = k * cos_t + pltpu.roll(k, HEAD_DIM, 1) * jnp.where(lane < HEAD_DIM, -sin_t, sin_t)
    k_rep = jnp.dot(k.astype(BF16), kcopy_ref[...], preferred_element_type=F32)
    v_rep = jnp.dot(v.astype(BF16), vcopy_ref[...], preferred_element_type=F32)
    for kh in range(KV_HEADS):
        kd_ref[kh] = k_rep[:, kh * 2 * LANES:(kh + 1) * 2 * LANES].astype(BF16)
        vd_ref[kh] = v_rep[:, kh * LANES:(kh + 1) * LANES].astype(BF16)


def _layer_a(x2d, mod_a, a_norm, w_in, ln_g, ln_b, ws_m, bs_b, w_out, kv_norm, w_kv, cos_c, sin_c, batch, seq):
    n = x2d.shape[0]
    tiles = seq // ROW_TILE
    row = lambda b, i: (b * tiles + i, 0)
    o = np.arange(KV_HEADS * 2 * LANES)
    k_copy = _lane_copy_matrix(((o % (2 * LANES)) // LANES) * HEAD_DIM + (o // (2 * LANES)) * HALF_DIM + o % HALF_DIM)
    o = np.arange(KV_HEADS * LANES)
    v_copy = _lane_copy_matrix((o // LANES) * HEAD_DIM + o % HEAD_DIM)
    return pl.pallas_call(
        _layer_a_kernel,
        grid=(batch, tiles),
        in_specs=[
            pl.BlockSpec((ROW_TILE, D_MODEL), row),
            pl.BlockSpec((1, 1, 3 * D_MODEL), lambda b, i: (b, 0, 0)),
            _const_spec((1, D_MODEL)),
            _const_spec((3, A_GROUPS, D_MODEL, A_GROUP_DIM)),
            _const_spec((A_GROUPS, 1, A_GROUP_DIM)),
            _const_spec((A_GROUPS, 1, A_GROUP_DIM)),
            _const_spec((A_GROUPS, CHUNK, CHUNK)),
            _const_spec((A_GROUPS, CHUNK, LANES)),
            _const_spec((A_WIDTH, D_MODEL)),
            _const_spec((1, D_MODEL)),
            _const_spec((D_MODEL, 2 * LANES)),
            _const_spec(k_copy.shape),
            _const_spec(v_copy.shape),
            pl.BlockSpec((TABLE_ROWS, LANES), row),
            pl.BlockSpec((TABLE_ROWS, LANES), row),
        ],
        out_specs=[
            pl.BlockSpec((ROW_TILE, D_MODEL), row),
            pl.BlockSpec((KV_HEADS, ROW_TILE, 2 * LANES), lambda b, i: (0, b * tiles + i, 0)),
            pl.BlockSpec((KV_HEADS, ROW_TILE, LANES), lambda b, i: (0, b * tiles + i, 0)),
        ],
        out_shape=[
            jax.ShapeDtypeStruct((n, D_MODEL), F32),
            jax.ShapeDtypeStruct((KV_HEADS, n, 2 * LANES), BF16),
            jax.ShapeDtypeStruct((KV_HEADS, n, LANES), BF16),
        ],
        scratch_shapes=[
            pltpu.VMEM((ROW_TILE, D_MODEL), BF16),
            pltpu.VMEM((A_GROUPS, ROW_TILE, A_GROUP_DIM), F32),
            pltpu.VMEM((ROW_TILE, A_WIDTH), BF16),
            pltpu.VMEM((ROW_TILE, LANES), F32),
            pltpu.VMEM((ROW_TILE, LANES), F32),
        ],
        compiler_params=pltpu.CompilerParams(
            dimension_semantics=("arbitrary", "arbitrary"), vmem_limit_bytes=VMEM_LIMIT_BYTES),
        name="layer_a",
    )(x2d, mod_a, a_norm, w_in, ln_g, ln_b, ws_m, bs_b, w_out, kv_norm, w_kv, k_copy, v_copy, cos_c, sin_c)


def _layer_b_kernel(sinks_ref, x_ref, mod_ref, bnorm_ref, wq_ref, wz_ref, kc_ref, kp_ref, vc_ref, vp_ref,
                    cosc_ref, sinc_ref, wout_ref, fnorm_ref, out_ref,
                    h_scr, q_scr, k_scr, v_scr, o_scr, zs_scr, oz_scr, op_scr, cos_scr, sin_scr):
    tile = pl.program_id(1)
    mod = mod_ref[0]
    shift = mod[:, 0:D_MODEL]
    scale = mod[:, D_MODEL:2 * D_MODEL]
    gate = mod[:, 2 * D_MODEL:3 * D_MODEL]
    gs = bnorm_ref[...] * (1.0 + scale)
    for s in range(B_SUBTILES):
        x = x_ref[s * ROW_TILE:(s + 1) * ROW_TILE, :]
        rstd = lax.rsqrt(jnp.mean(x * x, axis=-1, keepdims=True) + EPS)
        h_scr[s] = ((x * rstd) * gs + shift).astype(BF16)
        cos_t, sin_t = _expand_tables(cosc_ref[s * TABLE_ROWS:(s + 1) * TABLE_ROWS, :],
                                      sinc_ref[s * TABLE_ROWS:(s + 1) * TABLE_ROWS, :])
        qk_scale = LOG2_E / math.sqrt(HEAD_DIM)
        cos_scr[s] = cos_t * qk_scale
        sin_scr[s] = sin_t * qk_scale

    k_scr[:, 0:WINDOW, :] = kp_ref[...]
    k_scr[:, WINDOW:, :] = kc_ref[...]
    v_scr[:, 0:WINDOW, :] = vp_ref[...]
    v_scr[:, WINDOW:, :] = vc_ref[...]

    group = (lax.broadcasted_iota(jnp.int32, (1, 2 * LANES), 1) & (LANES - 1)) // HALF_DIM
    keep = [(group == j).astype(BF16) for j in range(B_CHUNK_HEADS)]

    def q_chunk(s, c):
        qc = jnp.dot(h_scr[s], wq_ref[c], preferred_element_type=F32)
        cos_t = cos_scr[s]
        sin_t = sin_scr[s]
        x1 = qc[:, :LANES]
        x2 = qc[:, LANES:]
        qr = jnp.concatenate([x1 * cos_t - x2 * sin_t, x2 * cos_t + x1 * sin_t], axis=1).astype(BF16)
        for j in range(B_CHUNK_HEADS):
            q_scr[(s * B_COL_CHUNKS + c) * B_CHUNK_HEADS + j] = qr * keep[j]

    def z_chunk(s, c):
        zc = jnp.dot(h_scr[s], wz_ref[c], preferred_element_type=F32)
        zs_scr[s * B_COL_CHUNKS + c] = jax.nn.silu(zc)

    qi = lax.broadcasted_iota(jnp.int32, (WINDOW, WINDOW), 0)
    sj = lax.broadcasted_iota(jnp.int32, (WINDOW, WINDOW), 1)
    from_prev = sj > qi
    olow = lax.broadcasted_iota(jnp.int32, (WINDOW, LANES), 1) < HEAD_DIM
    olow_row = lax.broadcasted_iota(jnp.int32, (1, LANES), 1) < HEAD_DIM
    dn = (((1,), (1,)), ((), ()))

    def attend(s, row0, prev_penalty):
        rows = pl.ds(row0, WINDOW)
        key_rows = pl.ds(s * ROW_TILE + row0, 2 * WINDOW)
        for kh in range(KV_HEADS):
            kb = k_scr[kh, key_rows, :]
            vb = v_scr[kh, key_rows, :]
            qb = jnp.concatenate(
                [q_scr[s * Q_HEADS + kh * Q_PER_KV + g, rows, :] for g in range(Q_PER_KV)], axis=0)
            logits = lax.dot_general(qb, kb, dn, preferred_element_type=F32)
            probs, stats = [], []
            for g in range(Q_PER_KV):
                lg_prev = logits[g * WINDOW:(g + 1) * WINDOW, :WINDOW]
                if prev_penalty is not None:
                    lg_prev = lg_prev + prev_penalty
                lg = jnp.where(from_prev, lg_prev, logits[g * WINDOW:(g + 1) * WINDOW, WINDOW:])
                sink = sinks_ref[kh * Q_PER_KV + g] * LOG2_E
                m = jnp.maximum(jnp.max(lg, axis=-1, keepdims=True), sink)
                e = jnp.exp2(lg - m)
                stats.append((sink, m, jnp.sum(e, axis=-1, keepdims=True)))
                probs.append(jnp.concatenate(
                    [jnp.where(from_prev, e, 0.0), jnp.where(from_prev, 0.0, e)], axis=1).astype(BF16))
            pv = jnp.dot(jnp.concatenate(probs, axis=0), vb, preferred_element_type=F32)
            for p in range(Q_PER_KV // 2):
                (sink_a, m_a, sum_a), (sink_b, m_b, sum_b) = stats[2 * p], stats[2 * p + 1]
                sink_ab = jnp.where(olow_row, sink_a, sink_b)
                den = jnp.where(olow, sum_a, sum_b) + jnp.exp2(sink_ab - jnp.where(olow, m_a, m_b))
                num = jnp.where(olow, pv[(2 * p) * WINDOW:(2 * p + 1) * WINDOW],
                                pv[(2 * p + 1) * WINDOW:(2 * p + 2) * WINDOW])
                col = (kh * Q_PER_KV // 2 + p) * LANES
                o_scr[s, rows, col:col + LANES] = num * (1.0 / den)

    def gated(s):
        return [(o_scr[s, :, c * B_COL_CHUNK:(c + 1) * B_COL_CHUNK] * zs_scr[s * B_COL_CHUNKS + c]).astype(BF16)
                for c in range(B_COL_CHUNKS)]

    def finish(s, proj):
        x = x_ref[s * ROW_TILE:(s + 1) * ROW_TILE, :]
        x2 = x + gate * proj
        rstd2 = lax.rsqrt(jnp.mean(x2 * x2, axis=-1, keepdims=True) + EPS)
        out_ref[s * ROW_TILE:(s + 1) * ROW_TILE, :] = (x2 * rstd2) * fnorm_ref[...]

    for c in range(B_COL_CHUNKS):
        q_chunk(0, c)

    def first_body(r, carry):
        penalty = jnp.where(jnp.logical_and(tile == 0, r == 0), -1e30, 0.0)
        attend(0, pl.multiple_of(r * WINDOW, WINDOW), penalty)
        q_chunk(1, r)
        z_chunk(0, r)
        return carry

    lax.fori_loop(0, ROW_TILE // WINDOW, first_body, 0, unroll=True)
    oz_scr[...] = jnp.concatenate(gated(0), axis=1)

    def second_body(r, carry):
        attend(1, pl.multiple_of(r * WINDOW, WINDOW), None)
        op_scr[r] = jnp.dot(oz_scr[...], wout_ref[r], preferred_element_type=F32)
        z_chunk(1, r)
        return carry

    lax.fori_loop(0, ROW_TILE // WINDOW, second_body, 0, unroll=True)
    finish(0, jnp.concatenate([op_scr[c] for c in range(B_COL_CHUNKS)], axis=1))
    oz1 = jnp.concatenate(gated(1), axis=1)
    finish(1, jnp.concatenate(
        [jnp.dot(oz1, wout_ref[c], preferred_element_type=F32) for c in range(B_COL_CHUNKS)], axis=1))


def _layer_b(x1, mod_b, b_norm, wq, wz, kd, vd, cos_c, sin_c, sinks, w_out, final_norm, batch, seq):
    n = x1.shape[0]
    assert ROW_TILE // WINDOW == B_COL_CHUNKS, "row-block loops also walk the projection column chunks"
    tiles = seq // B_ROWS
    blocks_per_tile = B_ROWS // WINDOW
    row = lambda b, i: (b * tiles + i, 0)
    cur = lambda b, i: (0, b * tiles + i, 0)
    prev = lambda b, i: (0, jnp.maximum((b * tiles + i) * blocks_per_tile - 1, 0), 0)
    chunked = (B_COL_CHUNKS, D_MODEL, B_COL_CHUNK)
    return pl.pallas_call(
        _layer_b_kernel,
        grid=(batch, tiles),
        in_specs=[
            pl.BlockSpec(memory_space=pltpu.SMEM),
            pl.BlockSpec((B_ROWS, D_MODEL), row),
            pl.BlockSpec((1, 1, 3 * D_MODEL), lambda b, i: (b, 0, 0)),
            _const_spec((1, D_MODEL)),
            _const_spec(chunked),
            _const_spec(chunked),
            pl.BlockSpec((KV_HEADS, B_ROWS, 2 * LANES), cur),
            pl.BlockSpec((KV_HEADS, WINDOW, 2 * LANES), prev),
            pl.BlockSpec((KV_HEADS, B_ROWS, LANES), cur),
            pl.BlockSpec((KV_HEADS, WINDOW, LANES), prev),
            pl.BlockSpec((B_SUBTILES * TABLE_ROWS, LANES), row),
            pl.BlockSpec((B_SUBTILES * TABLE_ROWS, LANES), row),
            _const_spec(chunked),
            _const_spec((1, D_MODEL)),
        ],
        out_specs=pl.BlockSpec((B_ROWS, D_MODEL), row),
        out_shape=jax.ShapeDtypeStruct((n, D_MODEL), F32),
        scratch_shapes=[
            pltpu.VMEM((B_SUBTILES, ROW_TILE, D_MODEL), BF16),
            pltpu.VMEM((B_SUBTILES * Q_HEADS, ROW_TILE, 2 * LANES), BF16),
            pltpu.VMEM((KV_HEADS, B_ROWS + WINDOW, 2 * LANES), BF16),
            pltpu.VMEM((KV_HEADS, B_ROWS + WINDOW, LANES), BF16),
            pltpu.VMEM((B_SUBTILES, ROW_TILE, D_MODEL), F32),
            pltpu.VMEM((B_SUBTILES * B_COL_CHUNKS, ROW_TILE, B_COL_CHUNK), F32),
            pltpu.VMEM((ROW_TILE, D_MODEL), BF16),
            pltpu.VMEM((B_COL_CHUNKS, ROW_TILE, B_COL_CHUNK), F32),
            pltpu.VMEM((B_SUBTILES, ROW_TILE, LANES), F32),
            pltpu.VMEM((B_SUBTILES, ROW_TILE, LANES), F32),
        ],
        compiler_params=pltpu.CompilerParams(
            dimension_semantics=("arbitrary", "arbitrary"), vmem_limit_bytes=VMEM_LIMIT_BYTES),
        name="layer_b",
    )(sinks, x1, mod_b, b_norm, wq, wz, kd, kd, vd, vd, cos_c, sin_c, w_out, final_norm)


def kernel(x, c, positions, a_norm, a_ada_w, a_ada_b, a_w_in, a_ln_g, a_ln_b, a_w_s, a_b_s, a_w_out,
           kv_norm, w_kv, b_norm, b_ada_w, b_ada_b, b_w_in, b_sinks, b_w_out, final_norm):
    batch, seq, d = x.shape
    assert d == D_MODEL and seq % B_ROWS == 0
    assert a_norm.shape[0] == 1 and b_norm.shape[0] == 1, "one layer of each mixer"
    n = batch * seq

    mod_a, mod_b, ws_m = _prep(c, a_ada_w, a_ada_b, b_ada_w, b_ada_b, a_w_s)
    cos_c, sin_c = _rope_tables(positions)

    w_in = a_w_in[0].reshape(D_MODEL, 3, A_GROUPS, A_GROUP_DIM).transpose(1, 2, 0, 3).astype(BF16)
    w_out_a = a_w_out[0].astype(BF16)
    ln_g = a_ln_g[0].reshape(A_GROUPS, 1, A_GROUP_DIM)
    ln_b = a_ln_b[0].reshape(A_GROUPS, 1, A_GROUP_DIM)
    bs_b = jnp.broadcast_to(a_b_s[0][:, :, None], (A_GROUPS, CHUNK, LANES))
    col_chunks = lambda w: w.reshape(D_MODEL, B_COL_CHUNKS, B_COL_CHUNK).transpose(1, 0, 2).astype(BF16)
    wq = b_w_in[0][:, :D_MODEL].reshape(D_MODEL, B_COL_CHUNKS, B_CHUNK_HEADS, 2, HALF_DIM)
    wq = wq.transpose(1, 0, 3, 2, 4).reshape(B_COL_CHUNKS, D_MODEL, B_COL_CHUNK).astype(BF16)
    wz = col_chunks(b_w_in[0][:, D_MODEL:])
    wk = w_kv[:, :LANES].reshape(D_MODEL, KV_HEADS, 2, HALF_DIM).transpose(0, 2, 1, 3).reshape(D_MODEL, LANES)
    w_kv_p = jnp.concatenate([wk, w_kv[:, LANES:]], axis=1).astype(BF16)

    x1, kd, vd = _layer_a(
        x.reshape(n, D_MODEL), mod_a.reshape(batch, 1, 3 * D_MODEL), a_norm, w_in, ln_g, ln_b, ws_m, bs_b,
        w_out_a, kv_norm.reshape(1, D_MODEL), w_kv_p, cos_c, sin_c, batch, seq)
    out = _layer_b(
        x1, mod_b.reshape(batch, 1, 3 * D_MODEL), b_norm, wq, wz, kd, vd, cos_c, sin_c, b_sinks[0],
        col_chunks(b_w_out[0]), final_norm.reshape(1, D_MODEL), batch, seq)
    return out.reshape(batch, seq, D_MODEL)
```

```python
import functools
import math

import jax
import jax.numpy as jnp
import numpy as np
from jax import lax
from jax.experimental import pallas as pl
from jax.experimental.pallas import tpu as pltpu

F32 = jnp.float32
BF16 = jnp.bfloat16

D_MODEL = 1024
A_WIDTH = 2048
A_GROUPS = 8
A_GROUP_DIM = A_WIDTH // A_GROUPS
CHUNK = 128
HEAD_DIM = 64
HALF_DIM = HEAD_DIM // 2
Q_HEADS = 16
KV_HEADS = 2
Q_PER_KV = Q_HEADS // KV_HEADS
WINDOW = 128
ROPE_THETA = 10000.0
EPS = 1e-6
LOG2_E = math.log2(math.e)

LANES = 128
ROW_TILE = 512
POS_PER_ROW = LANES // HALF_DIM
TABLE_ROWS = ROW_TILE // POS_PER_ROW
B_SUBTILES = 2
B_ROWS = B_SUBTILES * ROW_TILE
B_COL_CHUNK = 256
B_COL_CHUNKS = D_MODEL // B_COL_CHUNK
B_CHUNK_HEADS = B_COL_CHUNK // HEAD_DIM
VMEM_LIMIT_BYTES = 60 * 1024 * 1024


def _const_spec(shape):
    zeros = (0,) * len(shape)
    return pl.BlockSpec(shape, lambda *_: zeros, pipeline_mode=pl.Buffered(1))


def _prep_kernel(c_ref, wa_ref, ba_ref, wb_ref, bb_ref, ws_ref, moda_ref, modb_ref, wsm_ref):
    sc = jax.nn.silu(c_ref[...])
    moda_ref[...] = jnp.dot(sc, wa_ref[0], preferred_element_type=F32,
                            precision=lax.Precision.HIGHEST) + ba_ref[0]
    modb_ref[...] = jnp.dot(sc, wb_ref[0], preferred_element_type=F32,
                            precision=lax.Precision.HIGHEST) + bb_ref[0]
    t = lax.broadcasted_iota(jnp.int32, (CHUNK, CHUNK), 0)
    s = lax.broadcasted_iota(jnp.int32, (CHUNK, CHUNK), 1)
    causal = (s <= t).astype(F32)
    for g in range(A_GROUPS):
        wsm_ref[g] = (ws_ref[0, g] * causal).astype(BF16)


def _prep(c, a_ada_w, a_ada_b, b_ada_w, b_ada_b, a_w_s):
    batch = c.shape[0]
    col = 1024
    n_col = 3 * D_MODEL // col
    return pl.pallas_call(
        _prep_kernel,
        grid=(n_col,),
        in_specs=[
            pl.BlockSpec((batch, D_MODEL), lambda j: (0, 0)),
            pl.BlockSpec((1, D_MODEL, col), lambda j: (0, 0, j)),
            pl.BlockSpec((1, 1, col), lambda j: (0, 0, j)),
            pl.BlockSpec((1, D_MODEL, col), lambda j: (0, 0, j)),
            pl.BlockSpec((1, 1, col), lambda j: (0, 0, j)),
            pl.BlockSpec((1, A_GROUPS, CHUNK, CHUNK), lambda j: (0, 0, 0, 0)),
        ],
        out_specs=[
            pl.BlockSpec((batch, col), lambda j: (0, j)),
            pl.BlockSpec((batch, col), lambda j: (0, j)),
            pl.BlockSpec((A_GROUPS, CHUNK, CHUNK), lambda j: (0, 0, 0)),
        ],
        out_shape=[
            jax.ShapeDtypeStruct((batch, 3 * D_MODEL), F32),
            jax.ShapeDtypeStruct((batch, 3 * D_MODEL), F32),
            jax.ShapeDtypeStruct((A_GROUPS, CHUNK, CHUNK), BF16),
        ],
        compiler_params=pltpu.CompilerParams(dimension_semantics=("arbitrary",)),
        name="prep",
    )(c, a_ada_w, a_ada_b.reshape(1, 1, -1), b_ada_w, b_ada_b.reshape(1, 1, -1), a_w_s)


def _rope_kernel(pos_ref, invf_ref, cos_ref, sin_ref):
    ang = pos_ref[...].astype(F32) * invf_ref[...]
    cos_ref[...] = jnp.cos(ang)
    sin_ref[...] = jnp.sin(ang)


def _rope_tables(positions):
    n = positions.size
    inv_freq = ROPE_THETA ** (-jnp.arange(0, HEAD_DIM, 2, dtype=F32) / HEAD_DIM)
    invf = jnp.tile(inv_freq, POS_PER_ROW).reshape(1, LANES)
    pos = positions.reshape(n // ROW_TILE, POS_PER_ROW, TABLE_ROWS)
    pos = jnp.repeat(jnp.swapaxes(pos, 1, 2), HALF_DIM, axis=2).reshape(n // POS_PER_ROW, LANES)
    rows = 2048
    return pl.pallas_call(
        _rope_kernel,
        grid=(pos.shape[0] // rows,),
        in_specs=[pl.BlockSpec((rows, LANES), lambda i: (i, 0)),
                  pl.BlockSpec((1, LANES), lambda i: (0, 0))],
        out_specs=[pl.BlockSpec((rows, LANES), lambda i: (i, 0)),
                   pl.BlockSpec((rows, LANES), lambda i: (i, 0))],
        out_shape=[jax.ShapeDtypeStruct(pos.shape, F32), jax.ShapeDtypeStruct(pos.shape, F32)],
        compiler_params=pltpu.CompilerParams(dimension_semantics=("arbitrary",)),
        name="rope_tables",
    )(pos, invf)


def _expand_tables(cos_c, sin_c):
    lane = lax.broadcasted_iota(jnp.int32, (TABLE_ROWS, LANES), 1)

    def spread(t):
        t = jnp.where(lane < HALF_DIM, t, pltpu.roll(t, HALF_DIM, 1))
        return jnp.where(lane < HEAD_DIM, t, pltpu.roll(t, HEAD_DIM, 1))

    cos_parts, sin_parts = [], []
    for a in range(POS_PER_ROW):
        shift = (LANES - HALF_DIM * a) % LANES
        cos_parts.append(spread(cos_c if shift == 0 else pltpu.roll(cos_c, shift, 1)))
        sin_parts.append(spread(sin_c if shift == 0 else pltpu.roll(sin_c, shift, 1)))
    return jnp.concatenate(cos_parts, axis=0), jnp.concatenate(sin_parts, axis=0)


def _lane_copy_matrix(src_lane_of_out):
    src = np.asarray(src_lane_of_out)
    e = np.zeros((LANES, src.size), np.float32)
    e[src, np.arange(src.size)] = 1.0
    return jnp.asarray(e, dtype=BF16)


def _layer_a_kernel(x_ref, mod_ref, anorm_ref, win_ref, lng_ref, lnb_ref, ws_ref, bs_ref, wout_ref,
                    kvn_ref, wkv_ref, kcopy_ref, vcopy_ref, cosc_ref, sinc_ref,
                    x1_ref, kd_ref, vd_ref,
                    h_scr, gv_scr, y_scr, s1_scr, s2_scr, x1_scr):
    @pl.when(pl.program_id(0) == 0)
    def _():
        x1_scr[...] = jnp.zeros_like(x1_scr)

    x1p = x1_scr[...]
    rstd2 = lax.rsqrt(jnp.mean(x1p * x1p, axis=-1, keepdims=True) + EPS)
    hk = ((x1p * rstd2) * kvn_ref[...]).astype(BF16)
    kv = jnp.dot(hk, wkv_ref[...], preferred_element_type=F32)
    k = kv[:, :LANES]
    v = kv[:, LANES:]
    cos_t, sin_t = _expand_tables(cosc_ref[...], sinc_ref[...])
    lane = lax.broadcasted_iota(jnp.int32, k.shape, 1)
    k = k * cos_t + pltpu.roll(k, HEAD_DIM, 1) * jnp.where(lane < HEAD_DIM, -sin_t, sin_t)
    k_rep = jnp.dot(k.astype(BF16), kcopy_ref[...], preferred_element_type=F32)
    v_rep = jnp.dot(v.astype(BF16), vcopy_ref[...], preferred_element_type=F32)
    for kh in range(KV_HEADS):
        kd_ref[kh] = k_rep[:, kh * 2 * LANES:(kh + 1) * 2 * LANES].astype(BF16)
        vd_ref[kh] = v_rep[:, kh * LANES:(kh + 1) * LANES].astype(BF16)

    mod = mod_ref[0]
    shift = mod[:, 0:D_MODEL]
    scale = mod[:, D_MODEL:2 * D_MODEL]
    gate = mod[:, 2 * D_MODEL:3 * D_MODEL]
    gs = anorm_ref[...] * (1.0 + scale)

    def w_in(part, g):
        col = part * A_WIDTH + g * A_GROUP_DIM
        return win_ref[:, col:col + A_GROUP_DIM]

    x = x_ref[...]
    rstd = lax.rsqrt(jnp.mean(x * x, axis=-1, keepdims=True) + EPS)
    h_scr[...] = ((x * rstd) * gs + shift).astype(BF16)

    s1 = s2 = None
    for g in range(A_GROUPS):
        v = jnp.dot(h_scr[...], w_in(1, g), preferred_element_type=F32)
        gv = jax.nn.gelu(v)
        gv_scr[g] = gv
        lo = gv[:, :LANES]
        hi = gv[:, LANES:]
        s1 = lo + hi if s1 is None else s1 + (lo + hi)
        s2 = lo * lo + hi * hi if s2 is None else s2 + (lo * lo + hi * hi)
    mu = jnp.sum(s1, axis=-1, keepdims=True) * (1.0 / A_WIDTH)
    var = jnp.sum(s2, axis=-1, keepdims=True) * (1.0 / A_WIDTH) - mu * mu
    s1_scr[...] = jnp.broadcast_to(mu, s1_scr.shape)
    s2_scr[...] = jnp.broadcast_to(lax.rsqrt(var + EPS), s2_scr.shape)

    for g in range(A_GROUPS):
        mu_b = s1_scr[...]
        rs_b = s2_scr[...]
        gv = gv_scr[g]
        lg = lng_ref[g]
        lb = lnb_ref[g]
        vn = jnp.concatenate(
            [((gv[:, :LANES] - mu_b) * rs_b) * lg[:, :LANES] + lb[:, :LANES],
             ((gv[:, LANES:] - mu_b) * rs_b) * lg[:, LANES:] + lb[:, LANES:]], axis=1).astype(BF16)
        h = h_scr[...]
        u = jnp.dot(h, w_in(0, g), preferred_element_type=F32)
        z = jnp.dot(h, w_in(2, g), preferred_element_type=F32)
        ws = ws_ref[g]
        bs = bs_ref[g]
        bs2 = jnp.concatenate([bs, bs], axis=1)
        for c in range(ROW_TILE // CHUNK):
            rows = slice(c * CHUNK, (c + 1) * CHUNK)
            mixed = jnp.dot(ws, vn[rows], preferred_element_type=F32) + bs2
            y = jax.nn.gelu(u[rows]) * mixed * jax.nn.silu(z[rows])
            y_scr[rows, g * A_GROUP_DIM:(g + 1) * A_GROUP_DIM] = y.astype(BF16)

    out = jnp.dot(y_scr[...], wout_ref[...], preferred_element_type=F32)
    x1 = x + gate * out
    x1_ref[...] = x1
    x1_scr[...] = x1


def _layer_a(x2d, mod_a, a_norm, w_in, ln_g, ln_b, ws_m, bs_b, w_out, kv_norm, w_kv, cos_c, sin_c, batch, seq):
    n = x2d.shape[0]
    tiles = n // ROW_TILE
    tiles_per_seq = seq // ROW_TILE
    row = lambda t: (jnp.minimum(t, tiles - 1), 0)
    prev_row = lambda t: (jnp.maximum(t - 1, 0), 0)
    prev_kv = lambda t: (0, jnp.maximum(t - 1, 0), 0)
    o = np.arange(KV_HEADS * 2 * LANES)
    k_copy = _lane_copy_matrix(((o % (2 * LANES)) // LANES) * HEAD_DIM + (o // (2 * LANES)) * HALF_DIM + o % HALF_DIM)
    o = np.arange(KV_HEADS * LANES)
    v_copy = _lane_copy_matrix((o // LANES) * HEAD_DIM + o % HEAD_DIM)
    return pl.pallas_call(
        _layer_a_kernel,
        grid=(tiles + 1,),
        in_specs=[
            pl.BlockSpec((ROW_TILE, D_MODEL), row),
            pl.BlockSpec((1, 1, 3 * D_MODEL), lambda t: (jnp.minimum(t, tiles - 1) // tiles_per_seq, 0, 0)),
            _const_spec((1, D_MODEL)),
            _const_spec((D_MODEL, 3 * A_WIDTH)),
            _const_spec((A_GROUPS, 1, A_GROUP_DIM)),
            _const_spec((A_GROUPS, 1, A_GROUP_DIM)),
            _const_spec((A_GROUPS, CHUNK, CHUNK)),
            _const_spec((A_GROUPS, CHUNK, LANES)),
            _const_spec((A_WIDTH, D_MODEL)),
            _const_spec((1, D_MODEL)),
            _const_spec((D_MODEL, 2 * LANES)),
            _const_spec(k_copy.shape),
            _const_spec(v_copy.shape),
            pl.BlockSpec((TABLE_ROWS, LANES), prev_row),
            pl.BlockSpec((TABLE_ROWS, LANES), prev_row),
        ],
        out_specs=[
            pl.BlockSpec((ROW_TILE, D_MODEL), row),
            pl.BlockSpec((KV_HEADS, ROW_TILE, 2 * LANES), prev_kv),
            pl.BlockSpec((KV_HEADS, ROW_TILE, LANES), prev_kv),
        ],
        out_shape=[
            jax.ShapeDtypeStruct((n, D_MODEL), F32),
            jax.ShapeDtypeStruct((KV_HEADS, n, 2 * LANES), BF16),
            jax.ShapeDtypeStruct((KV_HEADS, n, LANES), BF16),
        ],
        scratch_shapes=[
            pltpu.VMEM((ROW_TILE, D_MODEL), BF16),
            pltpu.VMEM((A_GROUPS, ROW_TILE, A_GROUP_DIM), F32),
            pltpu.VMEM((ROW_TILE, A_WIDTH), BF16),
            pltpu.VMEM((ROW_TILE, LANES), F32),
            pltpu.VMEM((ROW_TILE, LANES), F32),
            pltpu.VMEM((ROW_TILE, D_MODEL), F32),
        ],
        compiler_params=pltpu.CompilerParams(
            dimension_semantics=("arbitrary",), vmem_limit_bytes=VMEM_LIMIT_BYTES),
        name="layer_a",
    )(x2d, mod_a, a_norm, w_in, ln_g, ln_b, ws_m, bs_b, w_out, kv_norm, w_kv, k_copy, v_copy, cos_c, sin_c)


def _layer_b_kernel(sinks_ref, x_ref, mod_ref, bnorm_ref, wq_ref, wz_ref, kc_ref, kp_ref, vc_ref, vp_ref,
                    cosc_ref, sinc_ref, wout_ref, fnorm_ref, out_ref,
                    h_scr, q_scr, k_scr, v_scr, o_scr, zs_scr, oz_scr, op_scr, cos_scr, sin_scr):
    tile = pl.program_id(1)
    mod = mod_ref[0]
    shift = mod[:, 0:D_MODEL]
    scale = mod[:, D_MODEL:2 * D_MODEL]
    gate = mod[:, 2 * D_MODEL:3 * D_MODEL]
    gs = bnorm_ref[...] * (1.0 + scale)
    for s in range(B_SUBTILES):
        x = x_ref[s * ROW_TILE:(s + 1) * ROW_TILE, :]
        rstd = lax.rsqrt(jnp.mean(x * x, axis=-1, keepdims=True) + EPS)
        h_scr[s] = ((x * rstd) * gs + shift).astype(BF16)
        cos_t, sin_t = _expand_tables(cosc_ref[s * TABLE_ROWS:(s + 1) * TABLE_ROWS, :],
                                      sinc_ref[s * TABLE_ROWS:(s + 1) * TABLE_ROWS, :])
        qk_scale = LOG2_E / math.sqrt(HEAD_DIM)
        cos_scr[s] = cos_t * qk_scale
        sin_scr[s] = sin_t * qk_scale

    k_scr[:, 0:WINDOW, :] = kp_ref[...]
    k_scr[:, WINDOW:, :] = kc_ref[...]
    v_scr[:, 0:WINDOW, :] = vp_ref[...]
    v_scr[:, WINDOW:, :] = vc_ref[...]

    group = (lax.broadcasted_iota(jnp.int32, (1, 2 * LANES), 1) & (LANES - 1)) // HALF_DIM
    keep = [(group == j).astype(BF16) for j in range(B_CHUNK_HEADS)]

    def q_chunk(s, c):
        qc = jnp.dot(h_scr[s], wq_ref[c], preferred_element_type=F32)
        cos_t = cos_scr[s]
        sin_t = sin_scr[s]
        x1 = qc[:, :LANES]
        x2 = qc[:, LANES:]
        qr = jnp.concatenate([x1 * cos_t - x2 * sin_t, x2 * cos_t + x1 * sin_t], axis=1).astype(BF16)
        for j in range(B_CHUNK_HEADS):
            q_scr[(s * B_COL_CHUNKS + c) * B_CHUNK_HEADS + j] = qr * keep[j]

    def z_chunk(s, c):
        zc = jnp.dot(h_scr[s], wz_ref[c], preferred_element_type=F32)
        zs_scr[s * B_COL_CHUNKS + c] = jax.nn.silu(zc)

    qi = lax.broadcasted_iota(jnp.int32, (WINDOW, WINDOW), 0)
    sj = lax.broadcasted_iota(jnp.int32, (WINDOW, WINDOW), 1)
    from_prev = sj > qi
    olow = lax.broadcasted_iota(jnp.int32, (WINDOW, LANES), 1) < HEAD_DIM
    olow_row = lax.broadcasted_iota(jnp.int32, (1, LANES), 1) < HEAD_DIM
    dn = (((1,), (1,)), ((), ()))

    def attend(s, row0, prev_penalty):
        rows = pl.ds(row0, WINDOW)
        key_rows = pl.ds(s * ROW_TILE + row0, 2 * WINDOW)
        for kh in range(KV_HEADS):
            kb = k_scr[kh, key_rows, :]
            vb = v_scr[kh, key_rows, :]
            qb = jnp.concatenate(
                [q_scr[s * Q_HEADS + kh * Q_PER_KV + g, rows, :] for g in range(Q_PER_KV)], axis=0)
            logits = lax.dot_general(qb, kb, dn, preferred_element_type=F32)
            probs, stats = [], []
            for g in range(Q_PER_KV):
                lg_prev = logits[g * WINDOW:(g + 1) * WINDOW, :WINDOW]
                if prev_penalty is not None:
                    lg_prev = lg_prev + prev_penalty
                lg = jnp.where(from_prev, lg_prev, logits[g * WINDOW:(g + 1) * WINDOW, WINDOW:])
                sink = sinks_ref[kh * Q_PER_KV + g] * LOG2_E
                m = jnp.maximum(jnp.max(lg, axis=-1, keepdims=True), sink)
                e = jnp.exp2(lg - m)
                stats.append((sink, m, jnp.sum(e, axis=-1, keepdims=True)))
                probs.append(jnp.concatenate(
                    [jnp.where(from_prev, e, 0.0), jnp.where(from_prev, 0.0, e)], axis=1).astype(BF16))
            pv = jnp.dot(jnp.concatenate(probs, axis=0), vb, preferred_element_type=F32)
            for p in range(Q_PER_KV // 2):
                (sink_a, m_a, sum_a), (sink_b, m_b, sum_b) = stats[2 * p], stats[2 * p + 1]
                sink_ab = jnp.where(olow_row, sink_a, sink_b)
                den = jnp.where(olow, sum_a, sum_b) + jnp.exp2(sink_ab - jnp.where(olow, m_a, m_b))
                num = jnp.where(olow, pv[(2 * p) * WINDOW:(2 * p + 1) * WINDOW],
                                pv[(2 * p + 1) * WINDOW:(2 * p + 2) * WINDOW])
                col = (kh * Q_PER_KV // 2 + p) * LANES
                o_scr[s, rows, col:col + LANES] = num * (1.0 / den)

    def gated(s):
        return [(o_scr[s, :, c * B_COL_CHUNK:(c + 1) * B_COL_CHUNK] * zs_scr[s * B_COL_CHUNKS + c]).astype(BF16)
                for c in range(B_COL_CHUNKS)]

    def finish(s, proj):
        x = x_ref[s * ROW_TILE:(s + 1) * ROW_TILE, :]
        x2 = x + gate * proj
        rstd2 = lax.rsqrt(jnp.mean(x2 * x2, axis=-1, keepdims=True) + EPS)
        out_ref[s * ROW_TILE:(s + 1) * ROW_TILE, :] = (x2 * rstd2) * fnorm_ref[...]

    for c in range(B_COL_CHUNKS):
        q_chunk(0, c)

    def first_body(r, carry):
        penalty = jnp.where(jnp.logical_and(tile == 0, r == 0), -1e30, 0.0)
        attend(0, pl.multiple_of(r * WINDOW, WINDOW), penalty)
        q_chunk(1, r)
        z_chunk(0, r)
        return carry

    lax.fori_loop(0, ROW_TILE // WINDOW, first_body, 0, unroll=True)
    oz_scr[...] = jnp.concatenate(gated(0), axis=1)

    def second_body(r, carry):
        attend(1, pl.multiple_of(r * WINDOW, WINDOW), None)
        op_scr[r] = jnp.dot(oz_scr[...], wout_ref[r], preferred_element_type=F32)
        z_chunk(1, r)
        return carry

    lax.fori_loop(0, ROW_TILE // WINDOW, second_body, 0, unroll=True)
    finish(0, jnp.concatenate([op_scr[c] for c in range(B_COL_CHUNKS)], axis=1))
    oz1 = jnp.concatenate(gated(1), axis=1)
    finish(1, jnp.concatenate(
        [jnp.dot(oz1, wout_ref[c], preferred_element_type=F32) for c in range(B_COL_CHUNKS)], axis=1))


def _layer_b(x1, mod_b, b_norm, wq, wz, kd, vd, cos_c, sin_c, sinks, w_out, final_norm, batch, seq):
    n = x1.shape[0]
    assert ROW_TILE // WINDOW == B_COL_CHUNKS, "row-block loops also walk the projection column chunks"
    tiles = seq // B_ROWS
    blocks_per_tile = B_ROWS // WINDOW
    row = lambda b, i: (b * tiles + i, 0)
    cur = lambda b, i: (0, b * tiles + i, 0)
    prev = lambda b, i: (0, jnp.maximum((b * tiles + i) * blocks_per_tile - 1, 0), 0)
    chunked = (B_COL_CHUNKS, D_MODEL, B_COL_CHUNK)
    return pl.pallas_call(
        _layer_b_kernel,
        grid=(batch, tiles),
        in_specs=[
            pl.BlockSpec(memory_space=pltpu.SMEM),
            pl.BlockSpec((B_ROWS, D_MODEL), row),
            pl.BlockSpec((1, 1, 3 * D_MODEL), lambda b, i: (b, 0, 0)),
            _const_spec((1, D_MODEL)),
            _const_spec(chunked),
            _const_spec(chunked),
            pl.BlockSpec((KV_HEADS, B_ROWS, 2 * LANES), cur),
            pl.BlockSpec((KV_HEADS, WINDOW, 2 * LANES), prev),
            pl.BlockSpec((KV_HEADS, B_ROWS, LANES), cur),
            pl.BlockSpec((KV_HEADS, WINDOW, LANES), prev),
            pl.BlockSpec((B_SUBTILES * TABLE_ROWS, LANES), row),
            pl.BlockSpec((B_SUBTILES * TABLE_ROWS, LANES), row),
            _const_spec(chunked),
            _const_spec((1, D_MODEL)),
        ],
        out_specs=pl.BlockSpec((B_ROWS, D_MODEL), row),
        out_shape=jax.ShapeDtypeStruct((n, D_MODEL), F32),
        scratch_shapes=[
            pltpu.VMEM((B_SUBTILES, ROW_TILE, D_MODEL), BF16),
            pltpu.VMEM((B_SUBTILES * Q_HEADS, ROW_TILE, 2 * LANES), BF16),
            pltpu.VMEM((KV_HEADS, B_ROWS + WINDOW, 2 * LANES), BF16),
            pltpu.VMEM((KV_HEADS, B_ROWS + WINDOW, LANES), BF16),
            pltpu.VMEM((B_SUBTILES, ROW_TILE, D_MODEL), F32),
            pltpu.VMEM((B_SUBTILES * B_COL_CHUNKS, ROW_TILE, B_COL_CHUNK), F32),
            pltpu.VMEM((ROW_TILE, D_MODEL), BF16),
            pltpu.VMEM((B_COL_CHUNKS, ROW_TILE, B_COL_CHUNK), F32),
            pltpu.VMEM((B_SUBTILES, ROW_TILE, LANES), F32),
            pltpu.VMEM((B_SUBTILES, ROW_TILE, LANES), F32),
        ],
        compiler_params=pltpu.CompilerParams(
            dimension_semantics=("arbitrary", "arbitrary"), vmem_limit_bytes=VMEM_LIMIT_BYTES),
        name="layer_b",
    )(sinks, x1, mod_b, b_norm, wq, wz, kd, kd, vd, vd, cos_c, sin_c, w_out, final_norm)


def kernel(x, c, positions, a_norm, a_ada_w, a_ada_b, a_w_in, a_ln_g, a_ln_b, a_w_s, a_b_s, a_w_out,
           kv_norm, w_kv, b_norm, b_ada_w, b_ada_b, b_w_in, b_sinks, b_w_out, final_norm):
    batch, seq, d = x.shape
    assert d == D_MODEL and seq % B_ROWS == 0
    assert a_norm.shape[0] == 1 and b_norm.shape[0] == 1, "one layer of each mixer"
    n = batch * seq

    mod_a, mod_b, ws_m = _prep(c, a_ada_w, a_ada_b, b_ada_w, b_ada_b, a_w_s)
    cos_c, sin_c = _rope_tables(positions)

    w_in = a_w_in[0].astype(BF16)
    w_out_a = a_w_out[0].astype(BF16)
    ln_g = a_ln_g[0].reshape(A_GROUPS, 1, A_GROUP_DIM)
    ln_b = a_ln_b[0].reshape(A_GROUPS, 1, A_GROUP_DIM)
    bs_b = jnp.broadcast_to(a_b_s[0][:, :, None], (A_GROUPS, CHUNK, LANES))
    col_chunks = lambda w: w.reshape(D_MODEL, B_COL_CHUNKS, B_COL_CHUNK).transpose(1, 0, 2).astype(BF16)
    wq = b_w_in[0][:, :D_MODEL].reshape(D_MODEL, B_COL_CHUNKS, B_CHUNK_HEADS, 2, HALF_DIM)
    wq = wq.transpose(1, 0, 3, 2, 4).reshape(B_COL_CHUNKS, D_MODEL, B_COL_CHUNK).astype(BF16)
    wz = col_chunks(b_w_in[0][:, D_MODEL:])
    wk = w_kv[:, :LANES].reshape(D_MODEL, KV_HEADS, 2, HALF_DIM).transpose(0, 2, 1, 3).reshape(D_MODEL, LANES)
    w_kv_p = jnp.concatenate([wk, w_kv[:, LANES:]], axis=1).astype(BF16)

    x1, kd, vd = _layer_a(
        x.reshape(n, D_MODEL), mod_a.reshape(batch, 1, 3 * D_MODEL), a_norm, w_in, ln_g, ln_b, ws_m, bs_b,
        w_out_a, kv_norm.reshape(1, D_MODEL), w_kv_p, cos_c, sin_c, batch, seq)
    out = _layer_b(
        x1, mod_b.reshape(batch, 1, 3 * D_MODEL), b_norm, wq, wz, kd, vd, cos_c, sin_c, b_sinks[0],
        col_chunks(b_w_out[0]), final_norm.reshape(1, D_MODEL), batch, seq)
    return out.reshape(batch, seq, D_MODEL)
```

```python
import functools
import math

import jax
import jax.numpy as jnp
import numpy as np
from jax import lax
from jax.experimental import pallas as pl
from jax.experimental.pallas import tpu as pltpu

F32 = jnp.float32
BF16 = jnp.bfloat16

D_MODEL = 1024
A_WIDTH = 2048
A_GROUPS = 8
A_GROUP_DIM = A_WIDTH // A_GROUPS
CHUNK = 128
HEAD_DIM = 64
HALF_DIM = HEAD_DIM // 2
Q_HEADS = 16
KV_HEADS = 2
Q_PER_KV = Q_HEADS // KV_HEADS
WINDOW = 128
ROPE_THETA = 10000.0
EPS = 1e-6
LOG2_E = math.log2(math.e)
QK_SCALE = LOG2_E / math.sqrt(HEAD_DIM)

LANES = 128
ROW_TILE = 512
POS_PER_ROW = LANES // HALF_DIM
TABLE_ROWS = ROW_TILE // POS_PER_ROW
B_SUBTILES = 2
B_ROWS = B_SUBTILES * ROW_TILE
B_COL_CHUNK = 256
B_COL_CHUNKS = D_MODEL // B_COL_CHUNK
B_CHUNK_HEADS = B_COL_CHUNK // HEAD_DIM
VMEM_LIMIT_BYTES = 60 * 1024 * 1024


def _const_spec(shape):
    zeros = (0,) * len(shape)
    return pl.BlockSpec(shape, lambda *_: zeros, pipeline_mode=pl.Buffered(1))


def _prep_kernel(c_ref, wa_ref, ba_ref, wb_ref, bb_ref, ws_ref, moda_ref, modb_ref, wsm_ref):
    sc = jax.nn.silu(c_ref[...])
    moda_ref[...] = jnp.dot(sc, wa_ref[0], preferred_element_type=F32,
                            precision=lax.Precision.HIGHEST) + ba_ref[0]
    modb_ref[...] = jnp.dot(sc, wb_ref[0], preferred_element_type=F32,
                            precision=lax.Precision.HIGHEST) + bb_ref[0]
    t = lax.broadcasted_iota(jnp.int32, (CHUNK, CHUNK), 0)
    s = lax.broadcasted_iota(jnp.int32, (CHUNK, CHUNK), 1)
    causal = (s <= t).astype(F32)
    for g in range(A_GROUPS):
        wsm_ref[g] = (ws_ref[0, g] * causal).astype(BF16)


def _prep(c, a_ada_w, a_ada_b, b_ada_w, b_ada_b, a_w_s):
    batch = c.shape[0]
    col = 1024
    n_col = 3 * D_MODEL // col
    return pl.pallas_call(
        _prep_kernel,
        grid=(n_col,),
        in_specs=[
            pl.BlockSpec((batch, D_MODEL), lambda j: (0, 0)),
            pl.BlockSpec((1, D_MODEL, col), lambda j: (0, 0, j)),
            pl.BlockSpec((1, 1, col), lambda j: (0, 0, j)),
            pl.BlockSpec((1, D_MODEL, col), lambda j: (0, 0, j)),
            pl.BlockSpec((1, 1, col), lambda j: (0, 0, j)),
            pl.BlockSpec((1, A_GROUPS, CHUNK, CHUNK), lambda j: (0, 0, 0, 0)),
        ],
        out_specs=[
            pl.BlockSpec((batch, col), lambda j: (0, j)),
            pl.BlockSpec((batch, col), lambda j: (0, j)),
            pl.BlockSpec((A_GROUPS, CHUNK, CHUNK), lambda j: (0, 0, 0)),
        ],
        out_shape=[
            jax.ShapeDtypeStruct((batch, 3 * D_MODEL), F32),
            jax.ShapeDtypeStruct((batch, 3 * D_MODEL), F32),
            jax.ShapeDtypeStruct((A_GROUPS, CHUNK, CHUNK), BF16),
        ],
        compiler_params=pltpu.CompilerParams(dimension_semantics=("arbitrary",)),
        name="prep",
    )(c, a_ada_w, a_ada_b.reshape(1, 1, -1), b_ada_w, b_ada_b.reshape(1, 1, -1), a_w_s)


def _compact_positions(positions):
    n = positions.size
    inv_freq = ROPE_THETA ** (-jnp.arange(0, HEAD_DIM, 2, dtype=F32) / HEAD_DIM)
    invf = jnp.tile(inv_freq, POS_PER_ROW).reshape(1, LANES)
    pos = jnp.swapaxes(positions.reshape(n // ROW_TILE, POS_PER_ROW, TABLE_ROWS), 1, 2)
    pos = jnp.broadcast_to(pos[..., None], pos.shape + (HALF_DIM,)).reshape(n // POS_PER_ROW, LANES)
    return pos, invf


def _expand_tables(cos_c, sin_c):
    lane = lax.broadcasted_iota(jnp.int32, (TABLE_ROWS, LANES), 1)

    def spread(t):
        t = jnp.where(lane < HALF_DIM, t, pltpu.roll(t, HALF_DIM, 1))
        return jnp.where(lane < HEAD_DIM, t, pltpu.roll(t, HEAD_DIM, 1))

    cos_parts, sin_parts = [], []
    for a in range(POS_PER_ROW):
        shift = (LANES - HALF_DIM * a) % LANES
        cos_parts.append(spread(cos_c if shift == 0 else pltpu.roll(cos_c, shift, 1)))
        sin_parts.append(spread(sin_c if shift == 0 else pltpu.roll(sin_c, shift, 1)))
    return jnp.concatenate(cos_parts, axis=0), jnp.concatenate(sin_parts, axis=0)


def _lane_copy_matrix(src_lane_of_out):
    src = np.asarray(src_lane_of_out)
    e = np.zeros((LANES, src.size), np.float32)
    e[src, np.arange(src.size)] = 1.0
    return jnp.asarray(e, dtype=BF16)


def _layer_a_kernel(x_ref, mod_ref, anorm_ref, win_ref, lng_ref, lnb_ref, ws_ref, bs_ref, wout_ref,
                    kvn_ref, wkv_ref, kcopy_ref, vcopy_ref, pos_ref, invf_ref,
                    x1_ref, kd_ref, vd_ref, cosq_ref, sinq_ref,
                    h_scr, gv_scr, y_scr, s1_scr, s2_scr):
    ang = pos_ref[...].astype(F32) * invf_ref[...]
    cos_t, sin_t = _expand_tables(jnp.cos(ang), jnp.sin(ang))
    cosq_ref[...] = cos_t * QK_SCALE
    sinq_ref[...] = sin_t * QK_SCALE

    mod = mod_ref[0]
    shift = mod[:, 0:D_MODEL]
    scale = mod[:, D_MODEL:2 * D_MODEL]
    gate = mod[:, 2 * D_MODEL:3 * D_MODEL]
    gs = anorm_ref[...] * (1.0 + scale)

    def w_in(part, g):
        return win_ref[part, g]

    x = x_ref[...]
    rstd = lax.rsqrt(jnp.mean(x * x, axis=-1, keepdims=True) + EPS)
    h_scr[...] = ((x * rstd) * gs + shift).astype(BF16)

    s1 = s2 = None
    for g in range(A_GROUPS):
        v = jnp.dot(h_scr[...], w_in(1, g), preferred_element_type=F32)
        gv = jax.nn.gelu(v)
        gv_scr[g] = gv
        lo = gv[:, :LANES]
        hi = gv[:, LANES:]
        s1 = lo + hi if s1 is None else s1 + (lo + hi)
        s2 = lo * lo + hi * hi if s2 is None else s2 + (lo * lo + hi * hi)
    mu = jnp.sum(s1, axis=-1, keepdims=True) * (1.0 / A_WIDTH)
    var = jnp.sum(s2, axis=-1, keepdims=True) * (1.0 / A_WIDTH) - mu * mu
    s1_scr[...] = jnp.broadcast_to(mu, s1_scr.shape)
    s2_scr[...] = jnp.broadcast_to(lax.rsqrt(var + EPS), s2_scr.shape)

    for g in range(A_GROUPS):
        mu_b = s1_scr[...]
        rs_b = s2_scr[...]
        gv = gv_scr[g]
        lg = lng_ref[g]
        lb = lnb_ref[g]
        vn = jnp.concatenate(
            [((gv[:, :LANES] - mu_b) * rs_b) * lg[:, :LANES] + lb[:, :LANES],
             ((gv[:, LANES:] - mu_b) * rs_b) * lg[:, LANES:] + lb[:, LANES:]], axis=1).astype(BF16)
        h = h_scr[...]
        u = jnp.dot(h, w_in(0, g), preferred_element_type=F32)
        z = jnp.dot(h, w_in(2, g), preferred_element_type=F32)
        ws = ws_ref[g]
        bs = bs_ref[g]
        bs2 = jnp.concatenate([bs, bs], axis=1)
        for c in range(ROW_TILE // CHUNK):
            rows = slice(c * CHUNK, (c + 1) * CHUNK)
            mixed = jnp.dot(ws, vn[rows], preferred_element_type=F32) + bs2
            y = jax.nn.gelu(u[rows]) * mixed * jax.nn.silu(z[rows])
            y_scr[rows, g * A_GROUP_DIM:(g + 1) * A_GROUP_DIM] = y.astype(BF16)

    out = jnp.dot(y_scr[...], wout_ref[...], preferred_element_type=F32)
    x1 = x + gate * out
    x1_ref[...] = x1

    rstd2 = lax.rsqrt(jnp.mean(x1 * x1, axis=-1, keepdims=True) + EPS)
    hk = ((x1 * rstd2) * kvn_ref[...]).astype(BF16)
    kv = jnp.dot(hk, wkv_ref[...], preferred_element_type=F32)
    k = kv[:, :LANES]
    v = kv[:, LANES:]
    lane = lax.broadcasted_iota(jnp.int32, k.shape, 1)
    k = k * cos_t + pltpu.roll(k, HEAD_DIM, 1) * jnp.where(lane < HEAD_DIM, -sin_t, sin_t)
    k_rep = jnp.dot(k.astype(BF16), kcopy_ref[...], preferred_element_type=F32)
    v_rep = jnp.dot(v.astype(BF16), vcopy_ref[...], preferred_element_type=F32)
    for kh in range(KV_HEADS):
        kd_ref[kh] = k_rep[:, kh * 2 * LANES:(kh + 1) * 2 * LANES].astype(BF16)
        vd_ref[kh] = v_rep[:, kh * LANES:(kh + 1) * LANES].astype(BF16)


def _layer_a(x2d, mod_a, a_norm, w_in, ln_g, ln_b, ws_m, bs_b, w_out, kv_norm, w_kv, pos_c, invf, batch, seq):
    n = x2d.shape[0]
    tiles = seq // ROW_TILE
    row = lambda b, i: (b * tiles + i, 0)
    kv_row = lambda b, i: (0, b * tiles + i, 0)
    o = np.arange(KV_HEADS * 2 * LANES)
    k_copy = _lane_copy_matrix(((o % (2 * LANES)) // LANES) * HEAD_DIM + (o // (2 * LANES)) * HALF_DIM + o % HALF_DIM)
    o = np.arange(KV_HEADS * LANES)
    v_copy = _lane_copy_matrix((o // LANES) * HEAD_DIM + o % HEAD_DIM)
    return pl.pallas_call(
        _layer_a_kernel,
        grid=(batch, tiles),
        in_specs=[
            pl.BlockSpec((ROW_TILE, D_MODEL), row),
            pl.BlockSpec((1, 1, 3 * D_MODEL), lambda b, i: (b, 0, 0)),
            _const_spec((1, D_MODEL)),
            _const_spec((3, A_GROUPS, D_MODEL, A_GROUP_DIM)),
            _const_spec((A_GROUPS, 1, A_GROUP_DIM)),
            _const_spec((A_GROUPS, 1, A_GROUP_DIM)),
            _const_spec((A_GROUPS, CHUNK, CHUNK)),
            _const_spec((A_GROUPS, CHUNK, LANES)),
            _const_spec((A_WIDTH, D_MODEL)),
            _const_spec((1, D_MODEL)),
            _const_spec((D_MODEL, 2 * LANES)),
            _const_spec(k_copy.shape),
            _const_spec(v_copy.shape),
            pl.BlockSpec((TABLE_ROWS, LANES), row),
            _const_spec((1, LANES)),
        ],
        out_specs=[
            pl.BlockSpec((ROW_TILE, D_MODEL), row),
            pl.BlockSpec((KV_HEADS, ROW_TILE, 2 * LANES), kv_row),
            pl.BlockSpec((KV_HEADS, ROW_TILE, LANES), kv_row),
            pl.BlockSpec((ROW_TILE, LANES), row),
            pl.BlockSpec((ROW_TILE, LANES), row),
        ],
        out_shape=[
            jax.ShapeDtypeStruct((n, D_MODEL), F32),
            jax.ShapeDtypeStruct((KV_HEADS, n, 2 * LANES), BF16),
            jax.ShapeDtypeStruct((KV_HEADS, n, LANES), BF16),
            jax.ShapeDtypeStruct((n, LANES), F32),
            jax.ShapeDtypeStruct((n, LANES), F32),
        ],
        scratch_shapes=[
            pltpu.VMEM((ROW_TILE, D_MODEL), BF16),
            pltpu.VMEM((A_GROUPS, ROW_TILE, A_GROUP_DIM), F32),
            pltpu.VMEM((ROW_TILE, A_WIDTH), BF16),
            pltpu.VMEM((ROW_TILE, LANES), F32),
            pltpu.VMEM((ROW_TILE, LANES), F32),
        ],
        compiler_params=pltpu.CompilerParams(
            dimension_semantics=("arbitrary", "arbitrary"), vmem_limit_bytes=VMEM_LIMIT_BYTES),
        name="layer_a",
    )(x2d, mod_a, a_norm, w_in, ln_g, ln_b, ws_m, bs_b, w_out, kv_norm, w_kv, k_copy, v_copy, pos_c, invf)


def _layer_b_kernel(sinks_ref, x_ref, mod_ref, bnorm_ref, wq_ref, wz_ref, kc_ref, kp_ref, vc_ref, vp_ref,
                    cos_ref, sin_ref, wout_ref, fnorm_ref, out_ref,
                    h_scr, q_scr, k_scr, v_scr, o_scr, zs_scr, oz_scr, op_scr):
    tile = pl.program_id(1)
    mod = mod_ref[0]
    shift = mod[:, 0:D_MODEL]
    scale = mod[:, D_MODEL:2 * D_MODEL]
    gate = mod[:, 2 * D_MODEL:3 * D_MODEL]
    gs = bnorm_ref[...] * (1.0 + scale)
    for s in range(B_SUBTILES):
        x = x_ref[s * ROW_TILE:(s + 1) * ROW_TILE, :]
        rstd = lax.rsqrt(jnp.mean(x * x, axis=-1, keepdims=True) + EPS)
        h_scr[s] = ((x * rstd) * gs + shift).astype(BF16)

    k_scr[:, 0:WINDOW, :] = kp_ref[...]
    k_scr[:, WINDOW:, :] = kc_ref[...]
    v_scr[:, 0:WINDOW, :] = vp_ref[...]
    v_scr[:, WINDOW:, :] = vc_ref[...]

    group = (lax.broadcasted_iota(jnp.int32, (1, 2 * LANES), 1) & (LANES - 1)) // HALF_DIM
    keep = [(group == j).astype(BF16) for j in range(B_CHUNK_HEADS)]

    def q_chunk(s, c):
        qc = jnp.dot(h_scr[s], wq_ref[c], preferred_element_type=F32)
        cos_t = cos_ref[s * ROW_TILE:(s + 1) * ROW_TILE, :]
        sin_t = sin_ref[s * ROW_TILE:(s + 1) * ROW_TILE, :]
        x1 = qc[:, :LANES]
        x2 = qc[:, LANES:]
        qr = jnp.concatenate([x1 * cos_t - x2 * sin_t, x2 * cos_t + x1 * sin_t], axis=1).astype(BF16)
        for j in range(B_CHUNK_HEADS):
            q_scr[(s * B_COL_CHUNKS + c) * B_CHUNK_HEADS + j] = qr * keep[j]

    def z_chunk(s, c):
        zc = jnp.dot(h_scr[s], wz_ref[c], preferred_element_type=F32)
        zs_scr[s * B_COL_CHUNKS + c] = jax.nn.silu(zc)

    qi = lax.broadcasted_iota(jnp.int32, (WINDOW, WINDOW), 0)
    sj = lax.broadcasted_iota(jnp.int32, (WINDOW, WINDOW), 1)
    from_prev = sj > qi
    olow = lax.broadcasted_iota(jnp.int32, (WINDOW, LANES), 1) < HEAD_DIM
    olow_row = lax.broadcasted_iota(jnp.int32, (1, LANES), 1) < HEAD_DIM
    dn = (((1,), (1,)), ((), ()))

    def attend(s, row0, prev_penalty):
        rows = pl.ds(row0, WINDOW)
        key_rows = pl.ds(s * ROW_TILE + row0, 2 * WINDOW)
        for kh in range(KV_HEADS):
            kb = k_scr[kh, key_rows, :]
            vb = v_scr[kh, key_rows, :]
            qb = jnp.concatenate(
                [q_scr[s * Q_HEADS + kh * Q_PER_KV + g, rows, :] for g in range(Q_PER_KV)], axis=0)
            logits = lax.dot_general(qb, kb, dn, preferred_element_type=F32)
            probs, stats = [], []
            for g in range(Q_PER_KV):
                lg_prev = logits[g * WINDOW:(g + 1) * WINDOW, :WINDOW]
                if prev_penalty is not None:
                    lg_prev = lg_prev + prev_penalty
                lg = jnp.where(from_prev, lg_prev, logits[g * WINDOW:(g + 1) * WINDOW, WINDOW:])
                sink = sinks_ref[kh * Q_PER_KV + g] * LOG2_E
                m = jnp.maximum(jnp.max(lg, axis=-1, keepdims=True), sink)
                e = jnp.exp2(lg - m)
                stats.append((sink, m, jnp.sum(e, axis=-1, keepdims=True)))
                probs.append(jnp.concatenate(
                    [jnp.where(from_prev, e, 0.0), jnp.where(from_prev, 0.0, e)], axis=1).astype(BF16))
            pv = jnp.dot(jnp.concatenate(probs, axis=0), vb, preferred_element_type=F32)
            for p in range(Q_PER_KV // 2):
                (sink_a, m_a, sum_a), (sink_b, m_b, sum_b) = stats[2 * p], stats[2 * p + 1]
                sink_ab = jnp.where(olow_row, sink_a, sink_b)
                den = jnp.where(olow, sum_a, sum_b) + jnp.exp2(sink_ab - jnp.where(olow, m_a, m_b))
                num = jnp.where(olow, pv[(2 * p) * WINDOW:(2 * p + 1) * WINDOW],
                                pv[(2 * p + 1) * WINDOW:(2 * p + 2) * WINDOW])
                col = (kh * Q_PER_KV // 2 + p) * LANES
                o_scr[s, rows, col:col + LANES] = num * (1.0 / den)

    def gated(s):
        return [(o_scr[s, :, c * B_COL_CHUNK:(c + 1) * B_COL_CHUNK] * zs_scr[s * B_COL_CHUNKS + c]).astype(BF16)
                for c in range(B_COL_CHUNKS)]

    def finish(s, proj):
        x = x_ref[s * ROW_TILE:(s + 1) * ROW_TILE, :]
        x2 = x + gate * proj
        rstd2 = lax.rsqrt(jnp.mean(x2 * x2, axis=-1, keepdims=True) + EPS)
        out_ref[s * ROW_TILE:(s + 1) * ROW_TILE, :] = (x2 * rstd2) * fnorm_ref[...]

    for c in range(B_COL_CHUNKS):
        q_chunk(0, c)

    def first_body(r, carry):
        penalty = jnp.where(jnp.logical_and(tile == 0, r == 0), -1e30, 0.0)
        attend(0, pl.multiple_of(r * WINDOW, WINDOW), penalty)
        q_chunk(1, r)
        z_chunk(0, r)
        return carry

    lax.fori_loop(0, ROW_TILE // WINDOW, first_body, 0, unroll=True)
    oz_scr[...] = jnp.concatenate(gated(0), axis=1)

    def second_body(r, carry):
        attend(1, pl.multiple_of(r * WINDOW, WINDOW), None)
        op_scr[r] = jnp.dot(oz_scr[...], wout_ref[r], preferred_element_type=F32)
        z_chunk(1, r)
        return carry

    lax.fori_loop(0, ROW_TILE // WINDOW, second_body, 0, unroll=True)
    finish(0, jnp.concatenate([op_scr[c] for c in range(B_COL_CHUNKS)], axis=1))
    oz1 = jnp.concatenate(gated(1), axis=1)
    finish(1, jnp.concatenate(
        [jnp.dot(oz1, wout_ref[c], preferred_element_type=F32) for c in range(B_COL_CHUNKS)], axis=1))


def _layer_b(x1, mod_b, b_norm, wq, wz, kd, vd, cos_c, sin_c, sinks, w_out, final_norm, batch, seq):
    n = x1.shape[0]
    assert ROW_TILE // WINDOW == B_COL_CHUNKS, "row-block loops also walk the projection column chunks"
    tiles = seq // B_ROWS
    blocks_per_tile = B_ROWS // WINDOW
    row = lambda b, i: (b * tiles + i, 0)
    cur = lambda b, i: (0, b * tiles + i, 0)
    prev = lambda b, i: (0, jnp.maximum((b * tiles + i) * blocks_per_tile - 1, 0), 0)
    chunked = (B_COL_CHUNKS, D_MODEL, B_COL_CHUNK)
    return pl.pallas_call(
        _layer_b_kernel,
        grid=(batch, tiles),
        in_specs=[
            pl.BlockSpec(memory_space=pltpu.SMEM),
            pl.BlockSpec((B_ROWS, D_MODEL), row),
            pl.BlockSpec((1, 1, 3 * D_MODEL), lambda b, i: (b, 0, 0)),
            _const_spec((1, D_MODEL)),
            _const_spec(chunked),
            _const_spec(chunked),
            pl.BlockSpec((KV_HEADS, B_ROWS, 2 * LANES), cur),
            pl.BlockSpec((KV_HEADS, WINDOW, 2 * LANES), prev),
            pl.BlockSpec((KV_HEADS, B_ROWS, LANES), cur),
            pl.BlockSpec((KV_HEADS, WINDOW, LANES), prev),
            pl.BlockSpec((B_ROWS, LANES), row),
            pl.BlockSpec((B_ROWS, LANES), row),
            _const_spec(chunked),
            _const_spec((1, D_MODEL)),
        ],
        out_specs=pl.BlockSpec((B_ROWS, D_MODEL), row),
        out_shape=jax.ShapeDtypeStruct((n, D_MODEL), F32),
        scratch_shapes=[
            pltpu.VMEM((B_SUBTILES, ROW_TILE, D_MODEL), BF16),
            pltpu.VMEM((B_SUBTILES * Q_HEADS, ROW_TILE, 2 * LANES), BF16),
            pltpu.VMEM((KV_HEADS, B_ROWS + WINDOW, 2 * LANES), BF16),
            pltpu.VMEM((KV_HEADS, B_ROWS + WINDOW, LANES), BF16),
            pltpu.VMEM((B_SUBTILES, ROW_TILE, D_MODEL), F32),
            pltpu.VMEM((B_SUBTILES * B_COL_CHUNKS, ROW_TILE, B_COL_CHUNK), F32),
            pltpu.VMEM((ROW_TILE, D_MODEL), BF16),
            pltpu.VMEM((B_COL_CHUNKS, ROW_TILE, B_COL_CHUNK), F32),
        ],
        compiler_params=pltpu.CompilerParams(
            dimension_semantics=("arbitrary", "arbitrary"), vmem_limit_bytes=VMEM_LIMIT_BYTES),
        name="layer_b",
    )(sinks, x1, mod_b, b_norm, wq, wz, kd, kd, vd, vd, cos_c, sin_c, w_out, final_norm)


def kernel(x, c, positions, a_norm, a_ada_w, a_ada_b, a_w_in, a_ln_g, a_ln_b, a_w_s, a_b_s, a_w_out,
           kv_norm, w_kv, b_norm, b_ada_w, b_ada_b, b_w_in, b_sinks, b_w_out, final_norm):
    batch, seq, d = x.shape
    assert d == D_MODEL and seq % B_ROWS == 0
    assert a_norm.shape[0] == 1 and b_norm.shape[0] == 1, "one layer of each mixer"
    n = batch * seq

    mod_a, mod_b, ws_m = _prep(c, a_ada_w, a_ada_b, b_ada_w, b_ada_b, a_w_s)
    pos_c, invf = _compact_positions(positions)

    w_in = a_w_in[0].reshape(D_MODEL, 3, A_GROUPS, A_GROUP_DIM).transpose(1, 2, 0, 3).astype(BF16)
    w_out_a = a_w_out[0].astype(BF16)
    ln_g = a_ln_g[0].reshape(A_GROUPS, 1, A_GROUP_DIM)
    ln_b = a_ln_b[0].reshape(A_GROUPS, 1, A_GROUP_DIM)
    bs_b = jnp.broadcast_to(a_b_s[0][:, :, None], (A_GROUPS, CHUNK, LANES))
    col_chunks = lambda w: w.reshape(D_MODEL, B_COL_CHUNKS, B_COL_CHUNK).transpose(1, 0, 2).astype(BF16)
    wq = b_w_in[0][:, :D_MODEL].reshape(D_MODEL, B_COL_CHUNKS, B_CHUNK_HEADS, 2, HALF_DIM)
    wq = wq.transpose(1, 0, 3, 2, 4).reshape(B_COL_CHUNKS, D_MODEL, B_COL_CHUNK).astype(BF16)
    wz = col_chunks(b_w_in[0][:, D_MODEL:])
    wk = w_kv[:, :LANES].reshape(D_MODEL, KV_HEADS, 2, HALF_DIM).transpose(0, 2, 1, 3).reshape(D_MODEL, LANES)
    w_kv_p = jnp.concatenate([wk, w_kv[:, LANES:]], axis=1).astype(BF16)

    x1, kd, vd, cos_q, sin_q = _layer_a(
        x.reshape(n, D_MODEL), mod_a.reshape(batch, 1, 3 * D_MODEL), a_norm, w_in, ln_g, ln_b, ws_m, bs_b,
        w_out_a, kv_norm.reshape(1, D_MODEL), w_kv_p, pos_c, invf, batch, seq)
    out = _layer_b(
        x1, mod_b.reshape(batch, 1, 3 * D_MODEL), b_norm, wq, wz, kd, vd, cos_q, sin_q, b_sinks[0],
        col_chunks(b_w_out[0]), final_norm.reshape(1, D_MODEL), batch, seq)
    return out.reshape(batch, seq, D_MODEL)
```
